```python
import math
import jax, jax.numpy as jnp
from jax import lax
import numpy as np

D_MODEL = 2048
BATCH = 16
SEQ = 2048
DEPTH = 1

MEM_LEN = 256
D_FF = 5632
POOL_GROUPS = 4
POOL_GROUP_DIM = 128
POOL_WIDTH = POOL_GROUPS * POOL_GROUP_DIM
POOL_WINDOWS = (2, 4, 8, 16)
FOX_HEADS = 16
FOX_HEAD_DIM = 64
FOX_WIDTH = FOX_HEADS * FOX_HEAD_DIM
MEM_HEADS = 4
MEM_HEAD_DIM = 128
MEM_WIDTH = MEM_HEADS * MEM_HEAD_DIM
N_BRANCHES = 3
GATE_WIDTH = N_BRANCHES * D_MODEL
Q_BLOCK = 128
EPS = 1e-6
IN_SPLITS = (POOL_WIDTH, FOX_WIDTH, FOX_WIDTH, FOX_WIDTH, FOX_HEADS, MEM_WIDTH, GATE_WIDTH)
IN_WIDTH = sum(IN_SPLITS)

kernel_name = "hybrid_pool_fox_memxattn_macaron"


def rmsnorm(x, g):
    xf = x.astype(jnp.float32)
    y = xf * lax.rsqrt(jnp.mean(xf * xf, axis=-1, keepdims=True) + EPS)
    return (y * g.astype(jnp.float32)).astype(x.dtype)


def swiglu_half_ffn(x, norm_g, w_gate_up, w_down):
    h = rmsnorm(x, norm_g)
    gate, up = jnp.split(h @ w_gate_up, 2, axis=-1)
    return 0.5 * ((jax.nn.silu(gate) * up) @ w_down)


def causal_window_mean(u, w):
    S = u.shape[1]
    cs = jnp.cumsum(u.astype(jnp.float32), axis=1)
    lagged = jnp.pad(cs, ((0, 0), (w, 0), (0, 0)))[:, :S]
    count = jnp.minimum(jnp.arange(1, S + 1), w).astype(jnp.float32)
    return ((cs - lagged) / count[None, :, None]).astype(u.dtype)


def pool_mixer(u, pool_w, pool_scale):
    B, S, _ = u.shape
    groups = u.reshape(B, S, POOL_GROUPS, POOL_GROUP_DIM)
    pooled = jnp.stack([causal_window_mean(groups[:, :, g], POOL_WINDOWS[g])
                        for g in range(POOL_GROUPS)], axis=2)
    mixed = jnp.einsum('bsgc,gcd->bsgd', pooled - groups, pool_w)
    return mixed.reshape(B, S, POOL_WIDTH) * pool_scale


def forgetting_attention(q, k, v, log_f):
    B, S, H, Dh = q.shape
    c = jnp.cumsum(log_f, axis=1).transpose(0, 2, 1)
    scale = Dh ** -0.5
    outs = []
    for i in range(S // Q_BLOCK):
        q0, q1 = i * Q_BLOCK, (i + 1) * Q_BLOCK
        logits = jnp.einsum('bqhd,bkhd->bhqk', q[:, q0:q1], k[:, :q1]).astype(jnp.float32) * scale
        logits = logits + c[:, :, q0:q1, None] - c[:, :, None, :q1]
        causal = (q0 + jnp.arange(Q_BLOCK))[:, None] >= jnp.arange(q1)[None, :]
        logits = jnp.where(causal[None, None], logits, -jnp.inf)
        p = jax.nn.softmax(logits, axis=-1).astype(v.dtype)
        outs.append(jnp.einsum('bhqk,bkhd->bqhd', p, v[:, :q1]))
    return jnp.concatenate(outs, axis=1)


def memory_attention(q, k, v):
    scale = q.shape[-1] ** -0.5
    logits = jnp.einsum('bshd,bmhd->bhsm', q, k).astype(jnp.float32) * scale
    p = jax.nn.softmax(logits, axis=-1).astype(v.dtype)
    return jnp.einsum('bhsm,bmhd->bshd', p, v)


def setup_inputs(seed: int = 0) -> dict:
    key = jax.random.key(seed)
    ks = jax.random.split(key, 24)
    nrm = lambda k, shape, fan_in: jax.random.normal(k, shape, jnp.float32) * fan_in ** -0.5
    gain = lambda k, shape: 1.0 + 0.1 * jax.random.normal(k, shape, jnp.float32)
    L = DEPTH
    return {
        "x": jax.random.normal(ks[0], (BATCH, SEQ, D_MODEL), jnp.float32),
        "mem": jax.random.normal(ks[1], (BATCH, MEM_LEN, D_MODEL), jnp.float32),
        "ffn1_norm": gain(ks[2], (L, D_MODEL)),
        "ffn1_w_gate_up": nrm(ks[3], (L, D_MODEL, 2 * D_FF), D_MODEL),
        "ffn1_w_down": nrm(ks[4], (L, D_FF, D_MODEL), D_FF),
        "mix_norm": gain(ks[5], (L, D_MODEL)),
        "mem_norm": gain(ks[6], (L, D_MODEL)),
        "w_in": nrm(ks[7], (L, D_MODEL, IN_WIDTH), D_MODEL),
        "b_forget": 2.0 + 0.1 * jax.random.normal(ks[8], (L, FOX_HEADS), jnp.float32),
        "pool_w": nrm(ks[9], (L, POOL_GROUPS, POOL_GROUP_DIM, POOL_GROUP_DIM), POOL_GROUP_DIM),
        "pool_scale": gain(ks[10], (L, POOL_WIDTH)),
        "w_pool_up": nrm(ks[11], (L, POOL_WIDTH, D_MODEL), POOL_WIDTH),
        "fox_q_norm": gain(ks[12], (L, FOX_HEAD_DIM)),
        "fox_k_norm": gain(ks[13], (L, FOX_HEAD_DIM)),
        "w_fox_o": nrm(ks[14], (L, FOX_WIDTH, D_MODEL), FOX_WIDTH),
        "w_mem_kv": nrm(ks[15], (L, D_MODEL, 2 * MEM_WIDTH), D_MODEL),
        "mem_q_norm": gain(ks[16], (L, MEM_HEAD_DIM)),
        "mem_k_norm": gain(ks[17], (L, MEM_HEAD_DIM)),
        "w_mem_o": nrm(ks[18], (L, MEM_WIDTH, D_MODEL), MEM_WIDTH),
        "w_out": nrm(ks[19], (L, D_MODEL, D_MODEL), D_MODEL),
        "ffn2_norm": gain(ks[20], (L, D_MODEL)),
        "ffn2_w_gate_up": nrm(ks[21], (L, D_MODEL, 2 * D_FF), D_MODEL),
        "ffn2_w_down": nrm(ks[22], (L, D_FF, D_MODEL), D_FF),
    }


def reference(x, mem, ffn1_norm, ffn1_w_gate_up, ffn1_w_down, mix_norm, mem_norm, w_in,
              b_forget, pool_w, pool_scale, w_pool_up, fox_q_norm, fox_k_norm, w_fox_o,
              w_mem_kv, mem_q_norm, mem_k_norm, w_mem_o, w_out,
              ffn2_norm, ffn2_w_gate_up, ffn2_w_down):
    B, S, _ = x.shape
    M = mem.shape[1]
    split_idx = list(np.cumsum(IN_SPLITS)[:-1])
    for l in range(DEPTH):
        x = x + swiglu_half_ffn(x, ffn1_norm[l], ffn1_w_gate_up[l], ffn1_w_down[l])

        h = rmsnorm(x, mix_norm[l])
        u_pool, q_f, k_f, v_f, f_logit, q_m, gate_logit = jnp.split(h @ w_in[l], split_idx, axis=-1)

        y_pool = pool_mixer(u_pool, pool_w[l], pool_scale[l]) @ w_pool_up[l]

        q_f = rmsnorm(q_f.reshape(B, S, FOX_HEADS, FOX_HEAD_DIM), fox_q_norm[l])
        k_f = rmsnorm(k_f.reshape(B, S, FOX_HEADS, FOX_HEAD_DIM), fox_k_norm[l])
        v_f = v_f.reshape(B, S, FOX_HEADS, FOX_HEAD_DIM)
        log_f = jax.nn.log_sigmoid(f_logit.astype(jnp.float32) + b_forget[l].astype(jnp.float32))
        y_fox = forgetting_attention(q_f, k_f, v_f, log_f).reshape(B, S, FOX_WIDTH) @ w_fox_o[l]

        k_m, v_m = jnp.split(rmsnorm(mem, mem_norm[l]) @ w_mem_kv[l], 2, axis=-1)
        q_m = rmsnorm(q_m.reshape(B, S, MEM_HEADS, MEM_HEAD_DIM), mem_q_norm[l])
        k_m = rmsnorm(k_m.reshape(B, M, MEM_HEADS, MEM_HEAD_DIM), mem_k_norm[l])
        v_m = v_m.reshape(B, M, MEM_HEADS, MEM_HEAD_DIM)
        y_mem = memory_attention(q_m, k_m, v_m).reshape(B, S, MEM_WIDTH) @ w_mem_o[l]

        g_pool, g_fox, g_mem = jnp.split(jax.nn.sigmoid(gate_logit), N_BRANCHES, axis=-1)
        merged = g_pool * y_pool + g_fox * y_fox + g_mem * y_mem
        x = x + merged @ w_out[l]

        x = x + swiglu_half_ffn(x, ffn2_norm[l], ffn2_w_gate_up[l], ffn2_w_down[l])
    return x
```

```python
import functools

import jax
import jax.numpy as jnp
from jax import lax
from jax.experimental import pallas as pl
from jax.experimental.pallas import tpu as pltpu

F32 = jnp.float32
BF16 = jnp.bfloat16

D_MODEL = 2048
D_FF = 5632
POOL_GROUPS = 4
POOL_GROUP_DIM = 128
POOL_WIDTH = POOL_GROUPS * POOL_GROUP_DIM
POOL_WINDOWS = (2, 4, 8, 16)
POOL_HALO = 16
FOX_HEADS = 16
FOX_HEAD_DIM = 64
FOX_WIDTH = FOX_HEADS * FOX_HEAD_DIM
MEM_HEADS = 4
MEM_HEAD_DIM = 128
MEM_WIDTH = MEM_HEADS * MEM_HEAD_DIM
GATE_WIDTH = 3 * D_MODEL
EPS = 1e-6

LANES = 128
HEAD_PAIRS = FOX_WIDTH // LANES

MAIN_WIDTH = GATE_WIDTH + 3 * FOX_WIDTH + MEM_WIDTH
Q_COL, K_COL, V_COL = GATE_WIDTH, GATE_WIDTH + FOX_WIDTH, GATE_WIDTH + 2 * FOX_WIDTH
QM_COL = GATE_WIDTH + 3 * FOX_WIDTH
SIDE_WIDTH = POOL_WIDTH + LANES

VMEM_LIMIT = 56 * 1024 * 1024


def _rms_scale(x):
    return lax.rsqrt(jnp.mean(x * x, axis=-1, keepdims=True) + EPS)


def _const_spec(shape):
    n = len(shape)
    return pl.BlockSpec(shape, lambda *_: (0,) * n, pipeline_mode=pl.Buffered(1))


def _ffn_kernel(x_ref, g_ref, wg_ref, wu_ref, wd_ref, o_ref, h_ref):
    j = pl.program_id(1)

    @pl.when(j == 0)
    def _():
        x = x_ref[...]
        h_ref[...] = (x * _rms_scale(x) * g_ref[...]).astype(BF16)
        o_ref[...] = x

    h = h_ref[...]
    gate = jnp.dot(h, wg_ref[...], preferred_element_type=F32)
    up = jnp.dot(h, wu_ref[...], preferred_element_type=F32)
    act = (0.5 * gate * jax.nn.sigmoid(gate) * up).astype(BF16)
    o_ref[...] += jnp.dot(act, wd_ref[...], preferred_element_type=F32)


def _ffn(x, norm_g, w_gate_up, w_down, *, tm=512, tf=512):
    m = x.shape[0]
    nf = D_FF // tf
    return pl.pallas_call(
        _ffn_kernel,
        grid=(m // tm, nf),
        in_specs=[
            pl.BlockSpec((tm, D_MODEL), lambda i, j: (i, 0)),
            _const_spec((1, D_MODEL)),
            pl.BlockSpec((D_MODEL, tf), lambda i, j: (0, j)),
            pl.BlockSpec((D_MODEL, tf), lambda i, j: (0, j + nf)),
            pl.BlockSpec((tf, D_MODEL), lambda i, j: (j, 0)),
        ],
        out_specs=pl.BlockSpec((tm, D_MODEL), lambda i, j: (i, 0)),
        out_shape=jax.ShapeDtypeStruct((m, D_MODEL), F32),
        scratch_shapes=[pltpu.VMEM((tm, D_MODEL), BF16)],
        compiler_params=pltpu.CompilerParams(
            dimension_semantics=("parallel", "arbitrary"),
            vmem_limit_bytes=VMEM_LIMIT),
        name="ffn",
    )(x, norm_g, w_gate_up, w_gate_up, w_down)


def _inproj_kernel(x_ref, g_ref, w_ref, ws_ref, o_ref, os_ref, h_ref):
    j = pl.program_id(1)

    @pl.when(j == 0)
    def _():
        x = x_ref[...]
        h = (x * _rms_scale(x) * g_ref[...]).astype(BF16)
        h_ref[...] = h
        os_ref[...] = jnp.dot(h, ws_ref[...], preferred_element_type=F32)

    o_ref[...] = jnp.dot(h_ref[...], w_ref[...], preferred_element_type=F32).astype(BF16)


def _inproj(x, norm_g, w_main, w_side, *, tm=512, tn=512):
    m = x.shape[0]
    return pl.pallas_call(
        _inproj_kernel,
        grid=(m // tm, MAIN_WIDTH // tn),
        in_specs=[
            pl.BlockSpec((tm, D_MODEL), lambda i, j: (i, 0)),
            _const_spec((1, D_MODEL)),
            pl.BlockSpec((D_MODEL, tn), lambda i, j: (0, j)),
            _const_spec((D_MODEL, SIDE_WIDTH)),
        ],
        out_specs=[
            pl.BlockSpec((tm, tn), lambda i, j: (i, j)),
            pl.BlockSpec((tm, SIDE_WIDTH), lambda i, j: (i, 0)),
        ],
        out_shape=[
            jax.ShapeDtypeStruct((m, MAIN_WIDTH), BF16),
            jax.ShapeDtypeStruct((m, SIDE_WIDTH), F32),
        ],
        scratch_shapes=[pltpu.VMEM((tm, D_MODEL), BF16)],
        compiler_params=pltpu.CompilerParams(
            dimension_semantics=("parallel", "arbitrary"),
            vmem_limit_bytes=VMEM_LIMIT),
        name="inproj",
    )(x, norm_g, w_main, w_side)


def _memkv_kernel(m_ref, g_ref, w_ref, kg_ref, k_ref, v_ref):
    x = m_ref[...]
    h = (x * _rms_scale(x) * g_ref[...]).astype(BF16)
    kv = jnp.dot(h, w_ref[...], preferred_element_type=F32)
    for hd in range(MEM_HEADS):
        sl = slice(hd * MEM_HEAD_DIM, (hd + 1) * MEM_HEAD_DIM)
        kh = kv[:, sl]
        k_ref[:, sl] = (kh * _rms_scale(kh) * kg_ref[...]).astype(BF16)
    v_ref[...] = kv[:, MEM_WIDTH:].astype(BF16)


def _memkv(mem, norm_g, w_kv, k_gain, *, tm=512):
    m = mem.shape[0]
    return pl.pallas_call(
        _memkv_kernel,
        grid=(m // tm,),
        in_specs=[
            pl.BlockSpec((tm, D_MODEL), lambda i: (i, 0)),
            _const_spec((1, D_MODEL)),
            _const_spec((D_MODEL, 2 * MEM_WIDTH)),
            _const_spec((1, MEM_HEAD_DIM)),
        ],
        out_specs=[
            pl.BlockSpec((tm, MEM_WIDTH), lambda i: (i, 0)),
            pl.BlockSpec((tm, MEM_WIDTH), lambda i: (i, 0)),
        ],
        out_shape=[
            jax.ShapeDtypeStruct((m, MEM_WIDTH), BF16),
            jax.ShapeDtypeStruct((m, MEM_WIDTH), BF16),
        ],
        compiler_params=pltpu.CompilerParams(
            dimension_semantics=("parallel",), vmem_limit_bytes=VMEM_LIMIT),
        name="memkv",
    )(mem, norm_g, w_kv, k_gain)


CUM_CHUNK = 256


def _split3(x):
    hi = x.astype(BF16)
    r = x - hi.astype(F32)
    mid = r.astype(BF16)
    lo = (r - mid.astype(F32)).astype(BF16)
    return hi, mid, lo


def _forget_cumsum_kernel(f_ref, b_ref, o_ref):
    seq = f_ref.shape[0]
    row = lax.broadcasted_iota(jnp.int32, (CUM_CHUNK, CUM_CHUNK), 0)
    col = lax.broadcasted_iota(jnp.int32, (CUM_CHUNK, CUM_CHUNK), 1)
    tri = jnp.where(row >= col, 1.0, 0.0).astype(BF16)
    carry = jnp.zeros((1, LANES), F32)
    for ci in range(seq // CUM_CHUNK):
        z = f_ref[ci * CUM_CHUNK:(ci + 1) * CUM_CHUNK, :] + b_ref[...]
        logf = jnp.minimum(z, 0.0) - jnp.log(1.0 + jnp.exp(-jnp.abs(z)))
        c = carry
        for part in _split3(logf):
            c = c + jnp.dot(tri, part, preferred_element_type=F32)
        carry = c[CUM_CHUNK - 1:CUM_CHUNK, :]
        o_ref[:, ci * CUM_CHUNK:(ci + 1) * CUM_CHUNK] = c.T[:FOX_HEADS, :]


def _forget_cumsum(side, b_pad, *, batch, seq):
    side3 = side.reshape(batch, seq, SIDE_WIDTH)
    return pl.pallas_call(
        _forget_cumsum_kernel,
        grid=(batch,),
        in_specs=[
            pl.BlockSpec((None, seq, LANES), lambda b: (b, 0, POOL_WIDTH // LANES)),
            _const_spec((1, LANES)),
        ],
        out_specs=pl.BlockSpec((None, FOX_HEADS, seq), lambda b: (b, 0, 0)),
        out_shape=jax.ShapeDtypeStruct((batch, FOX_HEADS, seq), F32),
        compiler_params=pltpu.CompilerParams(
            dimension_semantics=("parallel",), vmem_limit_bytes=VMEM_LIMIT),
        name="forget_cumsum",
    )(side3, b_pad)


FOX_BLOCK = 256


def _pair_rms_scale(x, first_head):
    sq = x * x
    s_a = jnp.sum(jnp.where(first_head, sq, 0.0), axis=-1, keepdims=True)
    s_b = jnp.sum(jnp.where(first_head, 0.0, sq), axis=-1, keepdims=True)
    inv_a = lax.rsqrt(s_a * (1.0 / FOX_HEAD_DIM) + EPS)
    inv_b = lax.rsqrt(s_b * (1.0 / FOX_HEAD_DIM) + EPS)
    return jnp.where(first_head, inv_a, inv_b)


def _fox_kernel(q_ref, k_ref, v_ref, c_ref, qg_ref, kg_ref, o_ref, kn_ref):
    seq = q_ref.shape[0]
    blk = FOX_BLOCK
    nblk = seq // blk
    first_head = lax.broadcasted_iota(jnp.int32, (1, LANES), 1) < FOX_HEAD_DIM

    k = k_ref[...].astype(F32)
    kn_ref[...] = (k * _pair_rms_scale(k, first_head) * kg_ref[...]).astype(BF16)

    row = lax.broadcasted_iota(jnp.int32, (blk, blk), 0)
    col = lax.broadcasted_iota(jnp.int32, (blk, blk), 1)
    causal = row >= col
    nt = (((1,), (1,)), ((), ()))

    def q_tile(qi, _):
        q0 = pl.multiple_of(qi * blk, blk)
        q = q_ref[pl.ds(q0, blk), :].astype(F32)
        qn = q * _pair_rms_scale(q, first_head) * (qg_ref[...] * FOX_HEAD_DIM ** -0.5)
        q_heads = (jnp.where(first_head, qn, 0.0).astype(BF16),
                   jnp.where(first_head, 0.0, qn).astype(BF16))
        c_here = c_ref[qi]
        c_base = c_here[:, 0:1]

        def kv_step(j, carry, masked):
            k0 = pl.multiple_of(j * blk, blk)
            kb = kn_ref[pl.ds(k0, blk), :]
            vb = v_ref[pl.ds(k0, blk), :]
            decay = c_ref[j] - c_base
            out = []
            for hd in range(2):
                m_prev, l_prev, acc = carry[hd]
                s = lax.dot_general(q_heads[hd], kb, nt, preferred_element_type=F32)
                s = s - decay[hd:hd + 1, :]
                if masked:
                    s = jnp.where(causal, s, -jnp.inf)
                m_new = jnp.maximum(m_prev, jnp.max(s, axis=-1, keepdims=True))
                alpha = jnp.exp(m_prev - m_new)
                p = jnp.exp(s - m_new)
                l_new = alpha * l_prev + jnp.sum(p, axis=-1, keepdims=True)
                acc = alpha * acc + jnp.dot(p.astype(BF16), vb, preferred_element_type=F32)
                out.append((m_new, l_new, acc))
            return tuple(out)

        init = tuple((jnp.full((blk, 1), -jnp.inf, F32), jnp.zeros((blk, 1), F32),
                      jnp.zeros((blk, LANES), F32)) for _ in range(2))
        carry = lax.fori_loop(0, qi, functools.partial(kv_step, masked=False), init)
        (_, l_a, acc_a), (_, l_b, acc_b) = kv_step(qi, carry, masked=True)
        o = jnp.where(first_head, acc_a / l_a, acc_b / l_b)
        o_ref[pl.ds(q0, blk), :] = o.astype(BF16)
        return 0

    lax.fori_loop(0, nblk, q_tile, 0)


def _fox_attention(main3, c_blocks, q_gain2, k_gain2):
    batch, seq, _ = main3.shape
    nblk = seq // FOX_BLOCK
    qc, kc, vc = Q_COL // LANES, K_COL // LANES, V_COL // LANES
    return pl.pallas_call(
        _fox_kernel,
        grid=(batch, HEAD_PAIRS),
        in_specs=[
            pl.BlockSpec((None, seq, LANES), lambda b, p: (b, 0, qc + p)),
            pl.BlockSpec((None, seq, LANES), lambda b, p: (b, 0, kc + p)),
            pl.BlockSpec((None, seq, LANES), lambda b, p: (b, 0, vc + p)),
            pl.BlockSpec((None, None, nblk, 2, FOX_BLOCK), lambda b, p: (b, p, 0, 0, 0)),
            _const_spec((1, LANES)),
            _const_spec((1, LANES)),
        ],
        out_specs=pl.BlockSpec((None, seq, LANES), lambda b, p: (b, 0, p)),
        out_shape=jax.ShapeDtypeStruct((batch, seq, FOX_WIDTH), BF16),
        scratch_shapes=[pltpu.VMEM((seq, LANES), BF16)],
        compiler_params=pltpu.CompilerParams(
            dimension_semantics=("parallel", "parallel"),
            vmem_limit_bytes=VMEM_LIMIT),
        name="fox_attention",
    )(main3, main3, main3, c_blocks, q_gain2, k_gain2)


def _mix_kernel(seq, x_ref, u_ref, halo_ref, gp_ref, gf_ref, gm_ref, qm_ref,
                km_ref, vm_ref, of_ref, pw_ref, ps_ref, wpu_ref, wfo_ref,
                wmo_ref, wo_ref, qg_ref, o_ref, ubuf_ref):
    tm = x_ref.shape[0]
    pos0 = (pl.program_id(0) * tm) % seq

    u = u_ref[...]
    halo_valid = (pos0 > 0).astype(F32)
    ubuf_ref[0:POOL_HALO, :] = halo_ref[...] * halo_valid
    ubuf_ref[POOL_HALO:, :] = u
    pos = pos0 + lax.broadcasted_iota(jnp.int32, (tm, 1), 0)
    mixed = []
    for g, win in enumerate(POOL_WINDOWS):
        sl = slice(g * POOL_GROUP_DIM, (g + 1) * POOL_GROUP_DIM)
        tot = u[:, sl]
        for d in range(1, win):
            tot = tot + ubuf_ref[POOL_HALO - d:POOL_HALO - d + tm, sl]
        count = jnp.minimum(pos + 1, win).astype(F32)
        diff = tot / count - u[:, sl]
        mg = jnp.dot(diff.astype(BF16), pw_ref[g], preferred_element_type=F32)
        mixed.append((mg * ps_ref[:, sl]).astype(BF16))
    y_pool = jnp.dot(jnp.concatenate(mixed, axis=-1), wpu_ref[...],
                     preferred_element_type=F32)

    nt = (((1,), (1,)), ((), ()))
    heads = []
    for hd in range(MEM_HEADS):
        sl = slice(hd * MEM_HEAD_DIM, (hd + 1) * MEM_HEAD_DIM)
        q = qm_ref[:, sl].astype(F32)
        qn = (q * _rms_scale(q) * qg_ref[...]).astype(BF16)
        s = lax.dot_general(qn, km_ref[:, sl], nt, preferred_element_type=F32)
        s = s * (MEM_HEAD_DIM ** -0.5)
        p = jnp.exp(s - jnp.max(s, axis=-1, keepdims=True))
        p = p / jnp.sum(p, axis=-1, keepdims=True)
        heads.append(jnp.dot(p.astype(BF16), vm_ref[:, sl],
                             preferred_element_type=F32).astype(BF16))
    y_mem = jnp.dot(jnp.concatenate(heads, axis=-1), wmo_ref[...],
                    preferred_element_type=F32)

    y_fox = jnp.dot(of_ref[...], wfo_ref[...], preferred_element_type=F32)
    merged = (jax.nn.sigmoid(gp_ref[...].astype(F32)) * y_pool
              + jax.nn.sigmoid(gf_ref[...].astype(F32)) * y_fox
              + jax.nn.sigmoid(gm_ref[...].astype(F32)) * y_mem)
    o_ref[...] = x_ref[...] + jnp.dot(merged.astype(BF16), wo_ref[...],
                                      preferred_element_type=F32)


def _mix(x, side, main, km, vm, o_fox, pool_w, pool_scale, w_pool_up, w_fox_o,
         w_mem_o, w_out, q_gain, *, seq, tm=256):
    m = x.shape[0]
    mem_len = km.shape[1]
    halo_blocks = tm // POOL_HALO
    gate_blk = D_MODEL
    return pl.pallas_call(
        functools.partial(_mix_kernel, seq),
        grid=(m // tm,),
        in_specs=[
            pl.BlockSpec((tm, D_MODEL), lambda i: (i, 0)),
            pl.BlockSpec((tm, POOL_WIDTH), lambda i: (i, 0)),
            pl.BlockSpec((POOL_HALO, POOL_WIDTH),
                         lambda i: (jnp.maximum(i * halo_blocks - 1, 0), 0)),
            pl.BlockSpec((tm, gate_blk), lambda i: (i, 0)),
            pl.BlockSpec((tm, gate_blk), lambda i: (i, 1)),
            pl.BlockSpec((tm, gate_blk), lambda i: (i, 2)),
            pl.BlockSpec((tm, MEM_WIDTH), lambda i: (i, QM_COL // MEM_WIDTH)),
            pl.BlockSpec((None, mem_len, MEM_WIDTH), lambda i: ((i * tm) // seq, 0, 0)),
            pl.BlockSpec((None, mem_len, MEM_WIDTH), lambda i: ((i * tm) // seq, 0, 0)),
            pl.BlockSpec((tm, FOX_WIDTH), lambda i: (i, 0)),
            _const_spec((POOL_GROUPS, POOL_GROUP_DIM, POOL_GROUP_DIM)),
            _const_spec((1, POOL_WIDTH)),
            _const_spec((POOL_WIDTH, D_MODEL)),
            _const_spec((FOX_WIDTH, D_MODEL)),
            _const_spec((MEM_WIDTH, D_MODEL)),
            _const_spec((D_MODEL, D_MODEL)),
            _const_spec((1, MEM_HEAD_DIM)),
        ],
        out_specs=pl.BlockSpec((tm, D_MODEL), lambda i: (i, 0)),
        out_shape=jax.ShapeDtypeStruct((m, D_MODEL), F32),
        scratch_shapes=[pltpu.VMEM((tm + POOL_HALO, POOL_WIDTH), F32)],
        compiler_params=pltpu.CompilerParams(
            dimension_semantics=("parallel",), vmem_limit_bytes=VMEM_LIMIT),
        name="mix",
    )(x, side, side, main, main, main, main, km, vm, o_fox, pool_w, pool_scale,
      w_pool_up, w_fox_o, w_mem_o, w_out, q_gain)


def kernel(x, mem, ffn1_norm, ffn1_w_gate_up, ffn1_w_down, mix_norm, mem_norm, w_in,
           b_forget, pool_w, pool_scale, w_pool_up, fox_q_norm, fox_k_norm, w_fox_o,
           w_mem_kv, mem_q_norm, mem_k_norm, w_mem_o, w_out,
           ffn2_norm, ffn2_w_gate_up, ffn2_w_down):
    batch, seq, _ = x.shape
    mem_len = mem.shape[1]
    depth = w_in.shape[0]
    xf = x.reshape(batch * seq, D_MODEL)
    memf = mem.reshape(batch * mem_len, D_MODEL)
    row = lambda v: v.reshape(1, -1).astype(F32)

    c_q = POOL_WIDTH
    c_f = POOL_WIDTH + 3 * FOX_WIDTH
    c_qm = c_f + FOX_HEADS
    c_gate = c_qm + MEM_WIDTH

    for l in range(depth):
        wi = w_in[l]
        w_main = jnp.concatenate(
            [wi[:, c_gate:], wi[:, c_q:c_f], wi[:, c_qm:c_gate]], axis=1).astype(BF16)
        w_side = jnp.pad(jnp.concatenate([wi[:, :POOL_WIDTH], wi[:, c_f:c_qm]], axis=1),
                         ((0, 0), (0, LANES - FOX_HEADS))).astype(BF16)
        b_pad = jnp.pad(b_forget[l].astype(F32), (0, LANES - FOX_HEADS)).reshape(1, LANES)

        xf = _ffn(xf, row(ffn1_norm[l]), ffn1_w_gate_up[l].astype(BF16),
                  ffn1_w_down[l].astype(BF16))

        main, side = _inproj(xf, row(mix_norm[l]), w_main, w_side)
        km, vm = _memkv(memf, row(mem_norm[l]), w_mem_kv[l].astype(BF16),
                        row(mem_k_norm[l]))

        c_t = _forget_cumsum(side, b_pad, batch=batch, seq=seq)
        c_blocks = c_t.reshape(batch, HEAD_PAIRS, 2, seq // FOX_BLOCK, FOX_BLOCK)
        c_blocks = c_blocks.transpose(0, 1, 3, 2, 4)
        o_fox = _fox_attention(main.reshape(batch, seq, MAIN_WIDTH), c_blocks,
                               row(jnp.tile(fox_q_norm[l], 2)),
                               row(jnp.tile(fox_k_norm[l], 2)))

        xf = _mix(xf, side, main, km.reshape(batch, mem_len, MEM_WIDTH),
                  vm.reshape(batch, mem_len, MEM_WIDTH),
                  o_fox.reshape(batch * seq, FOX_WIDTH),
                  pool_w[l].astype(BF16), row(pool_scale[l]), w_pool_up[l].astype(BF16),
                  w_fox_o[l].astype(BF16), w_mem_o[l].astype(BF16), w_out[l].astype(BF16),
                  row(mem_q_norm[l]), seq=seq)

        xf = _ffn(xf, row(ffn2_norm[l]), ffn2_w_gate_up[l].astype(BF16),
                  ffn2_w_down[l].astype(BF16))
    return xf.reshape(batch, seq, D_MODEL)
```

```python
import functools

import jax
import jax.numpy as jnp
from jax import lax
from jax.experimental import pallas as pl
from jax.experimental.pallas import tpu as pltpu

F32 = jnp.float32
BF16 = jnp.bfloat16

D_MODEL = 2048
D_FF = 5632
POOL_GROUPS = 4
POOL_GROUP_DIM = 128
POOL_WIDTH = POOL_GROUPS * POOL_GROUP_DIM
POOL_WINDOWS = (2, 4, 8, 16)
POOL_HALO = 16
FOX_HEADS = 16
FOX_HEAD_DIM = 64
FOX_WIDTH = FOX_HEADS * FOX_HEAD_DIM
MEM_HEADS = 4
MEM_HEAD_DIM = 128
MEM_WIDTH = MEM_HEADS * MEM_HEAD_DIM
GATE_WIDTH = 3 * D_MODEL
EPS = 1e-6

LANES = 128
HEAD_PAIRS = FOX_WIDTH // LANES

MAIN_WIDTH = GATE_WIDTH + 3 * FOX_WIDTH + MEM_WIDTH
Q_COL, K_COL, V_COL = GATE_WIDTH, GATE_WIDTH + FOX_WIDTH, GATE_WIDTH + 2 * FOX_WIDTH
QM_COL = GATE_WIDTH + 3 * FOX_WIDTH
SIDE_WIDTH = POOL_WIDTH + LANES

VMEM_LIMIT = 56 * 1024 * 1024


def _rms_scale(x):
    return lax.rsqrt(jnp.mean(x * x, axis=-1, keepdims=True) + EPS)


def _const_spec(shape):
    n = len(shape)
    return pl.BlockSpec(shape, lambda *_: (0,) * n, pipeline_mode=pl.Buffered(1))


def _ffn_kernel(x_ref, g_ref, wg_ref, wu_ref, wd_ref, o_ref, h_ref):
    j = pl.program_id(1)

    @pl.when(j == 0)
    def _():
        x = x_ref[...]
        h_ref[...] = (x * _rms_scale(x) * g_ref[...]).astype(BF16)
        o_ref[...] = x

    h = h_ref[...]
    gate = jnp.dot(h, wg_ref[...], preferred_element_type=F32)
    up = jnp.dot(h, wu_ref[...], preferred_element_type=F32)
    act = (0.5 * gate * jax.nn.sigmoid(gate) * up).astype(BF16)
    o_ref[...] += jnp.dot(act, wd_ref[...], preferred_element_type=F32)


def _ffn(x, norm_g, w_gate_up, w_down, *, tm=1024, tf=512):
    m = x.shape[0]
    nf = D_FF // tf
    return pl.pallas_call(
        _ffn_kernel,
        grid=(m // tm, nf),
        in_specs=[
            pl.BlockSpec((tm, D_MODEL), lambda i, j: (i, 0),
                         pipeline_mode=pl.Buffered(1)),
            _const_spec((1, D_MODEL)),
            pl.BlockSpec((D_MODEL, tf), lambda i, j: (0, j)),
            pl.BlockSpec((D_MODEL, tf), lambda i, j: (0, j + nf)),
            pl.BlockSpec((tf, D_MODEL), lambda i, j: (j, 0)),
        ],
        out_specs=pl.BlockSpec((tm, D_MODEL), lambda i, j: (i, 0)),
        out_shape=jax.ShapeDtypeStruct((m, D_MODEL), F32),
        scratch_shapes=[pltpu.VMEM((tm, D_MODEL), BF16)],
        compiler_params=pltpu.CompilerParams(
            dimension_semantics=("parallel", "arbitrary"),
            vmem_limit_bytes=VMEM_LIMIT),
        name="ffn",
    )(x, norm_g, w_gate_up, w_gate_up, w_down)


def _inproj_kernel(x_ref, g_ref, w_ref, ws_ref, o_ref, os_ref, h_ref):
    j = pl.program_id(1)

    @pl.when(j == 0)
    def _():
        x = x_ref[...]
        h = (x * _rms_scale(x) * g_ref[...]).astype(BF16)
        h_ref[...] = h
        os_ref[...] = jnp.dot(h, ws_ref[...], preferred_element_type=F32)

    o_ref[...] = jnp.dot(h_ref[...], w_ref[...], preferred_element_type=F32).astype(BF16)


def _inproj(x, norm_g, w_main, w_side, *, tm=1024, tn=512):
    m = x.shape[0]
    return pl.pallas_call(
        _inproj_kernel,
        grid=(m // tm, MAIN_WIDTH // tn),
        in_specs=[
            pl.BlockSpec((tm, D_MODEL), lambda i, j: (i, 0)),
            _const_spec((1, D_MODEL)),
            pl.BlockSpec((D_MODEL, tn), lambda i, j: (0, j)),
            _const_spec((D_MODEL, SIDE_WIDTH)),
        ],
        out_specs=[
            pl.BlockSpec((tm, tn), lambda i, j: (i, j)),
            pl.BlockSpec((tm, SIDE_WIDTH), lambda i, j: (i, 0)),
        ],
        out_shape=[
            jax.ShapeDtypeStruct((m, MAIN_WIDTH), BF16),
            jax.ShapeDtypeStruct((m, SIDE_WIDTH), F32),
        ],
        scratch_shapes=[pltpu.VMEM((tm, D_MODEL), BF16)],
        compiler_params=pltpu.CompilerParams(
            dimension_semantics=("parallel", "arbitrary"),
            vmem_limit_bytes=VMEM_LIMIT),
        name="inproj",
    )(x, norm_g, w_main, w_side)


def _memkv_kernel(m_ref, g_ref, w_ref, kg_ref, k_ref, v_ref):
    x = m_ref[...]
    h = (x * _rms_scale(x) * g_ref[...]).astype(BF16)
    kv = jnp.dot(h, w_ref[...], preferred_element_type=F32)
    for hd in range(MEM_HEADS):
        sl = slice(hd * MEM_HEAD_DIM, (hd + 1) * MEM_HEAD_DIM)
        kh = kv[:, sl]
        k_ref[:, sl] = (kh * _rms_scale(kh) * kg_ref[...]).astype(BF16)
    v_ref[...] = kv[:, MEM_WIDTH:].astype(BF16)


def _memkv(mem, norm_g, w_kv, k_gain, *, tm=512):
    m = mem.shape[0]
    return pl.pallas_call(
        _memkv_kernel,
        grid=(m // tm,),
        in_specs=[
            pl.BlockSpec((tm, D_MODEL), lambda i: (i, 0)),
            _const_spec((1, D_MODEL)),
            _const_spec((D_MODEL, 2 * MEM_WIDTH)),
            _const_spec((1, MEM_HEAD_DIM)),
        ],
        out_specs=[
            pl.BlockSpec((tm, MEM_WIDTH), lambda i: (i, 0)),
            pl.BlockSpec((tm, MEM_WIDTH), lambda i: (i, 0)),
        ],
        out_shape=[
            jax.ShapeDtypeStruct((m, MEM_WIDTH), BF16),
            jax.ShapeDtypeStruct((m, MEM_WIDTH), BF16),
        ],
        compiler_params=pltpu.CompilerParams(
            dimension_semantics=("parallel",), vmem_limit_bytes=VMEM_LIMIT),
        name="memkv",
    )(mem, norm_g, w_kv, k_gain)


CUM_CHUNK = 256


def _split3(x):
    hi = x.astype(BF16)
    r = x - hi.astype(F32)
    mid = r.astype(BF16)
    lo = (r - mid.astype(F32)).astype(BF16)
    return hi, mid, lo


def _forget_cumsum_kernel(f_ref, b_ref, o_ref):
    seq = f_ref.shape[0]
    row = lax.broadcasted_iota(jnp.int32, (CUM_CHUNK, CUM_CHUNK), 0)
    col = lax.broadcasted_iota(jnp.int32, (CUM_CHUNK, CUM_CHUNK), 1)
    tri = jnp.where(row >= col, 1.0, 0.0).astype(BF16)
    carry = jnp.zeros((1, LANES), F32)
    for ci in range(seq // CUM_CHUNK):
        z = f_ref[ci * CUM_CHUNK:(ci + 1) * CUM_CHUNK, :] + b_ref[...]
        logf = jnp.minimum(z, 0.0) - jnp.log(1.0 + jnp.exp(-jnp.abs(z)))
        c = carry
        for part in _split3(logf):
            c = c + jnp.dot(tri, part, preferred_element_type=F32)
        carry = c[CUM_CHUNK - 1:CUM_CHUNK, :]
        o_ref[ci * CUM_CHUNK:(ci + 1) * CUM_CHUNK, :] = c


def _forget_cumsum(side, b_pad, *, batch, seq):
    side3 = side.reshape(batch, seq, SIDE_WIDTH)
    return pl.pallas_call(
        _forget_cumsum_kernel,
        grid=(batch,),
        in_specs=[
            pl.BlockSpec((None, seq, LANES), lambda b: (b, 0, POOL_WIDTH // LANES)),
            _const_spec((1, LANES)),
        ],
        out_specs=pl.BlockSpec((None, seq, LANES), lambda b: (b, 0, 0)),
        out_shape=jax.ShapeDtypeStruct((batch, seq, LANES), F32),
        compiler_params=pltpu.CompilerParams(
            dimension_semantics=("parallel",), vmem_limit_bytes=VMEM_LIMIT),
        name="forget_cumsum",
    )(side3, b_pad)


FOX_BLOCK = 256
N_PARTS = 3


def _fox_kernel(q_ref, k_ref, v_ref, c_ref, qg_ref, kg_ref, o_ref,
                qa_ref, qb_ref, ka_ref, kb_ref, va_ref, vb_ref):
    seq = q_ref.shape[0]
    blk = FOX_BLOCK
    nblk = seq // blk
    pair = pl.program_id(1)
    lane = lax.broadcasted_iota(jnp.int32, (1, LANES), 1)
    head_a = lane < FOX_HEAD_DIM
    head_b = jnp.logical_not(head_a)

    r = lax.broadcasted_iota(jnp.int32, (LANES, LANES), 0)
    c = lax.broadcasted_iota(jnp.int32, (LANES, LANES), 1)
    avg = jnp.where((r < FOX_HEAD_DIM) == (c < FOX_HEAD_DIM),
                    1.0 / FOX_HEAD_DIM, 0.0).astype(BF16)

    def head_norm(x, gain):
        sq = x * x
        hi = sq.astype(BF16)
        lo = (sq - hi.astype(F32)).astype(BF16)
        ms = (jnp.dot(hi, avg, preferred_element_type=F32)
              + jnp.dot(lo, avg, preferred_element_type=F32))
        return x * lax.rsqrt(ms + EPS) * gain

    parts = jnp.concatenate(_split3(c_ref[...]), axis=1)
    r3 = lax.broadcasted_iota(jnp.int32, (N_PARTS * LANES, LANES), 0)
    c3 = lax.broadcasted_iota(jnp.int32, (N_PARTS * LANES, LANES), 1)
    part_id = jnp.right_shift(r3, 7)
    src_lane = jnp.bitwise_and(r3, LANES - 1)

    def decay_features(head, base):
        mine = src_lane == head
        sel = jnp.where(mine & (c3 == base + part_id), 1.0,
                        jnp.where(mine & (c3 == base + N_PARTS + part_id), -1.0, 0.0))
        return jnp.dot(parts, sel.astype(BF16), preferred_element_type=F32)

    base_a, base_b = FOX_HEAD_DIM, 0
    feat_a = decay_features(2 * pair, base_a)
    feat_b = decay_features(2 * pair + 1, base_b)

    def augment(x, own, feat, base, query):
        plus = (lane >= base) & (lane < base + N_PARTS)
        minus = (lane >= base + N_PARTS) & (lane < base + 2 * N_PARTS)
        var, one = (plus, minus) if query else (minus, plus)
        extra = jnp.where(var, feat, jnp.where(one, 1.0, 0.0))
        return jnp.where(own, x, extra).astype(BF16)

    qn = head_norm(q_ref[...].astype(F32), qg_ref[...] * FOX_HEAD_DIM ** -0.5)
    kn = head_norm(k_ref[...].astype(F32), kg_ref[...])
    qa_ref[...] = augment(qn, head_a, feat_a, base_a, True)
    qb_ref[...] = augment(qn, head_b, feat_b, base_b, True)
    ka_ref[...] = augment(kn, head_a, feat_a, base_a, False)
    kb_ref[...] = augment(kn, head_b, feat_b, base_b, False)

    v = v_ref[...]
    zero = jnp.zeros_like(v)
    va_ref[:, 0:LANES] = jnp.where(head_a, v, zero)
    vb_ref[:, 0:LANES] = jnp.where(head_b, v, zero)
    va_ref[:, LANES:] = jnp.broadcast_to(jnp.where(head_a, 1.0, 0.0).astype(BF16),
                                         (seq, LANES))
    vb_ref[:, LANES:] = jnp.broadcast_to(jnp.where(head_b, 1.0, 0.0).astype(BF16),
                                         (seq, LANES))

    row = lax.broadcasted_iota(jnp.int32, (blk, blk), 0)
    col = lax.broadcasted_iota(jnp.int32, (blk, blk), 1)
    causal = row >= col
    nt = (((1,), (1,)), ((), ()))

    for qi in range(nblk):
        q_rows = slice(qi * blk, (qi + 1) * blk)
        past = slice(0, qi * blk)
        acc = None
        for q_aug, k_aug, v_aug in ((qa_ref, ka_ref, va_ref), (qb_ref, kb_ref, vb_ref)):
            q = q_aug[q_rows, :]
            s_diag = lax.dot_general(q, k_aug[q_rows, :], nt, preferred_element_type=F32)
            s_diag = jnp.where(causal, s_diag, -jnp.inf)
            m = jnp.max(s_diag, axis=-1, keepdims=True)
            if qi > 0:
                s_past = lax.dot_general(q, k_aug[past, :], nt, preferred_element_type=F32)
                m = jnp.maximum(m, jnp.max(s_past, axis=-1, keepdims=True))
                p_past = jnp.exp(s_past - m).astype(BF16)
                part = jnp.dot(p_past, v_aug[past, :], preferred_element_type=F32)
                acc = part if acc is None else acc + part
            p_diag = jnp.exp(s_diag - m).astype(BF16)
            part = jnp.dot(p_diag, v_aug[q_rows, :], preferred_element_type=F32)
            acc = part if acc is None else acc + part
        o_ref[q_rows, :] = (acc[:, :LANES] / acc[:, LANES:]).astype(BF16)


def _fox_attention(main3, c_all, q_gain2, k_gain2):
    batch, seq, _ = main3.shape
    qc, kc, vc = Q_COL // LANES, K_COL // LANES, V_COL // LANES
    tile = pltpu.VMEM((seq, LANES), BF16)
    wide = pltpu.VMEM((seq, 2 * LANES), BF16)
    return pl.pallas_call(
        _fox_kernel,
        grid=(batch, HEAD_PAIRS),
        in_specs=[
            pl.BlockSpec((None, seq, LANES), lambda b, p: (b, 0, qc + p)),
            pl.BlockSpec((None, seq, LANES), lambda b, p: (b, 0, kc + p)),
            pl.BlockSpec((None, seq, LANES), lambda b, p: (b, 0, vc + p)),
            pl.BlockSpec((None, seq, LANES), lambda b, p: (b, 0, 0)),
            _const_spec((1, LANES)),
            _const_spec((1, LANES)),
        ],
        out_specs=pl.BlockSpec((None, seq, LANES), lambda b, p: (b, 0, p)),
        out_shape=jax.ShapeDtypeStruct((batch, seq, FOX_WIDTH), BF16),
        scratch_shapes=[tile, tile, tile, tile, wide, wide],
        compiler_params=pltpu.CompilerParams(
            dimension_semantics=("parallel", "parallel"),
            vmem_limit_bytes=VMEM_LIMIT),
        name="fox_attention",
    )(main3, main3, main3, c_all, q_gain2, k_gain2)


def _mix_kernel(seq, x_ref, u_ref, halo_ref, gp_ref, gf_ref, gm_ref, qm_ref,
                km_ref, vm_ref, of_ref, pw_ref, ps_ref, wpu_ref, wfo_ref,
                wmo_ref, wo_ref, qg_ref, o_ref, ubuf_ref):
    tm = x_ref.shape[0]
    pos0 = (pl.program_id(0) * tm) % seq

    u = u_ref[...]
    halo_valid = (pos0 > 0).astype(F32)
    ubuf_ref[0:POOL_HALO, :] = halo_ref[...] * halo_valid
    ubuf_ref[POOL_HALO:, :] = u
    pos = pos0 + lax.broadcasted_iota(jnp.int32, (tm, 1), 0)
    mixed = []
    for g, win in enumerate(POOL_WINDOWS):
        sl = slice(g * POOL_GROUP_DIM, (g + 1) * POOL_GROUP_DIM)
        tot = u[:, sl]
        for d in range(1, win):
            tot = tot + ubuf_ref[POOL_HALO - d:POOL_HALO - d + tm, sl]
        count = jnp.minimum(pos + 1, win).astype(F32)
        diff = tot / count - u[:, sl]
        mg = jnp.dot(diff.astype(BF16), pw_ref[g], preferred_element_type=F32)
        mixed.append((mg * ps_ref[:, sl]).astype(BF16))
    y_pool = jnp.dot(jnp.concatenate(mixed, axis=-1), wpu_ref[...],
                     preferred_element_type=F32)

    nt = (((1,), (1,)), ((), ()))
    heads = []
    for hd in range(MEM_HEADS):
        sl = slice(hd * MEM_HEAD_DIM, (hd + 1) * MEM_HEAD_DIM)
        q = qm_ref[:, sl].astype(F32)
        qn = (q * _rms_scale(q) * qg_ref[...]).astype(BF16)
        s = lax.dot_general(qn, km_ref[:, sl], nt, preferred_element_type=F32)
        s = s * (MEM_HEAD_DIM ** -0.5)
        p = jnp.exp(s - jnp.max(s, axis=-1, keepdims=True))
        p = p / jnp.sum(p, axis=-1, keepdims=True)
        heads.append(jnp.dot(p.astype(BF16), vm_ref[:, sl],
                             preferred_element_type=F32).astype(BF16))
    y_mem = jnp.dot(jnp.concatenate(heads, axis=-1), wmo_ref[...],
                    preferred_element_type=F32)

    y_fox = jnp.dot(of_ref[...], wfo_ref[...], preferred_element_type=F32)
    merged = (jax.nn.sigmoid(gp_ref[...].astype(F32)) * y_pool
              + jax.nn.sigmoid(gf_ref[...].astype(F32)) * y_fox
              + jax.nn.sigmoid(gm_ref[...].astype(F32)) * y_mem)
    o_ref[...] = x_ref[...] + jnp.dot(merged.astype(BF16), wo_ref[...],
                                      preferred_element_type=F32)


def _mix(x, side, main, km, vm, o_fox, pool_w, pool_scale, w_pool_up, w_fox_o,
         w_mem_o, w_out, q_gain, *, seq, tm=256):
    m = x.shape[0]
    mem_len = km.shape[1]
    halo_blocks = tm // POOL_HALO
    gate_blk = D_MODEL
    return pl.pallas_call(
        functools.partial(_mix_kernel, seq),
        grid=(m // tm,),
        in_specs=[
            pl.BlockSpec((tm, D_MODEL), lambda i: (i, 0)),
            pl.BlockSpec((tm, POOL_WIDTH), lambda i: (i, 0)),
            pl.BlockSpec((POOL_HALO, POOL_WIDTH),
                         lambda i: (jnp.maximum(i * halo_blocks - 1, 0), 0)),
            pl.BlockSpec((tm, gate_blk), lambda i: (i, 0)),
            pl.BlockSpec((tm, gate_blk), lambda i: (i, 1)),
            pl.BlockSpec((tm, gate_blk), lambda i: (i, 2)),
            pl.BlockSpec((tm, MEM_WIDTH), lambda i: (i, QM_COL // MEM_WIDTH)),
            pl.BlockSpec((None, mem_len, MEM_WIDTH), lambda i: ((i * tm) // seq, 0, 0)),
            pl.BlockSpec((None, mem_len, MEM_WIDTH), lambda i: ((i * tm) // seq, 0, 0)),
            pl.BlockSpec((tm, FOX_WIDTH), lambda i: (i, 0)),
            _const_spec((POOL_GROUPS, POOL_GROUP_DIM, POOL_GROUP_DIM)),
            _const_spec((1, POOL_WIDTH)),
            _const_spec((POOL_WIDTH, D_MODEL)),
            _const_spec((FOX_WIDTH, D_MODEL)),
            _const_spec((MEM_WIDTH, D_MODEL)),
            _const_spec((D_MODEL, D_MODEL)),
            _const_spec((1, MEM_HEAD_DIM)),
        ],
        out_specs=pl.BlockSpec((tm, D_MODEL), lambda i: (i, 0)),
        out_shape=jax.ShapeDtypeStruct((m, D_MODEL), F32),
        scratch_shapes=[pltpu.VMEM((tm + POOL_HALO, POOL_WIDTH), F32)],
        compiler_params=pltpu.CompilerParams(
            dimension_semantics=("parallel",), vmem_limit_bytes=VMEM_LIMIT),
        name="mix",
    )(x, side, side, main, main, main, main, km, vm, o_fox, pool_w, pool_scale,
      w_pool_up, w_fox_o, w_mem_o, w_out, q_gain)


def kernel(x, mem, ffn1_norm, ffn1_w_gate_up, ffn1_w_down, mix_norm, mem_norm, w_in,
           b_forget, pool_w, pool_scale, w_pool_up, fox_q_norm, fox_k_norm, w_fox_o,
           w_mem_kv, mem_q_norm, mem_k_norm, w_mem_o, w_out,
           ffn2_norm, ffn2_w_gate_up, ffn2_w_down):
    batch, seq, _ = x.shape
    mem_len = mem.shape[1]
    depth = w_in.shape[0]
    xf = x.reshape(batch * seq, D_MODEL)
    memf = mem.reshape(batch * mem_len, D_MODEL)
    row = lambda v: v.reshape(1, -1).astype(F32)

    c_q = POOL_WIDTH
    c_f = POOL_WIDTH + 3 * FOX_WIDTH
    c_qm = c_f + FOX_HEADS
    c_gate = c_qm + MEM_WIDTH

    for l in range(depth):
        wi = w_in[l]
        w_main = jnp.concatenate(
            [wi[:, c_gate:], wi[:, c_q:c_f], wi[:, c_qm:c_gate]], axis=1).astype(BF16)
        w_side = jnp.pad(jnp.concatenate([wi[:, :POOL_WIDTH], wi[:, c_f:c_qm]], axis=1),
                         ((0, 0), (0, LANES - FOX_HEADS))).astype(BF16)
        b_pad = jnp.pad(b_forget[l].astype(F32), (0, LANES - FOX_HEADS)).reshape(1, LANES)

        xf = _ffn(xf, row(ffn1_norm[l]), ffn1_w_gate_up[l].astype(BF16),
                  ffn1_w_down[l].astype(BF16))

        main, side = _inproj(xf, row(mix_norm[l]), w_main, w_side)
        km, vm = _memkv(memf, row(mem_norm[l]), w_mem_kv[l].astype(BF16),
                        row(mem_k_norm[l]))

        c_all = _forget_cumsum(side, b_pad, batch=batch, seq=seq)
        o_fox = _fox_attention(main.reshape(batch, seq, MAIN_WIDTH), c_all,
                               row(jnp.tile(fox_q_norm[l], 2)),
                               row(jnp.tile(fox_k_norm[l], 2)))

        xf = _mix(xf, side, main, km.reshape(batch, mem_len, MEM_WIDTH),
                  vm.reshape(batch, mem_len, MEM_WIDTH),
                  o_fox.reshape(batch * seq, FOX_WIDTH),
                  pool_w[l].astype(BF16), row(pool_scale[l]), w_pool_up[l].astype(BF16),
                  w_fox_o[l].astype(BF16), w_mem_o[l].astype(BF16), w_out[l].astype(BF16),
                  row(mem_q_norm[l]), seq=seq)

        xf = _ffn(xf, row(ffn2_norm[l]), ffn2_w_gate_up[l].astype(BF16),
                  ffn2_w_down[l].astype(BF16))
    return xf.reshape(batch, seq, D_MODEL)
```

```python
import functools

import jax
import jax.numpy as jnp
from jax import lax
from jax.experimental import pallas as pl
from jax.experimental.pallas import tpu as pltpu

F32 = jnp.float32
BF16 = jnp.bfloat16

D_MODEL = 2048
D_FF = 5632
POOL_GROUPS = 4
POOL_GROUP_DIM = 128
POOL_WIDTH = POOL_GROUPS * POOL_GROUP_DIM
POOL_WINDOWS = (2, 4, 8, 16)
POOL_HALO = 16
FOX_HEADS = 16
FOX_HEAD_DIM = 64
FOX_WIDTH = FOX_HEADS * FOX_HEAD_DIM
MEM_HEADS = 4
MEM_HEAD_DIM = 128
MEM_WIDTH = MEM_HEADS * MEM_HEAD_DIM
GATE_WIDTH = 3 * D_MODEL
EPS = 1e-6

LANES = 128
HEAD_PAIRS = FOX_WIDTH // LANES

MAIN_WIDTH = GATE_WIDTH + 3 * FOX_WIDTH + MEM_WIDTH
Q_COL, K_COL, V_COL = GATE_WIDTH, GATE_WIDTH + FOX_WIDTH, GATE_WIDTH + 2 * FOX_WIDTH
QM_COL = GATE_WIDTH + 3 * FOX_WIDTH
SIDE_WIDTH = POOL_WIDTH + LANES

VMEM_LIMIT = 56 * 1024 * 1024


def _rms_scale(x):
    return lax.rsqrt(jnp.mean(x * x, axis=-1, keepdims=True) + EPS)


def _const_spec(shape):
    n = len(shape)
    return pl.BlockSpec(shape, lambda *_: (0,) * n, pipeline_mode=pl.Buffered(1))


def _ffn_kernel(x_ref, g_ref, wg_ref, wu_ref, wd_ref, o_ref, h_ref):
    j = pl.program_id(1)

    @pl.when(j == 0)
    def _():
        x = x_ref[...]
        h_ref[...] = (x * _rms_scale(x) * g_ref[...]).astype(BF16)
        o_ref[...] = x

    h = h_ref[...]
    gate = jnp.dot(h, wg_ref[...], preferred_element_type=F32)
    up = jnp.dot(h, wu_ref[...], preferred_element_type=F32)
    act = (0.5 * gate * jax.nn.sigmoid(gate) * up).astype(BF16)
    o_ref[...] += jnp.dot(act, wd_ref[...], preferred_element_type=F32)


def _ffn(x, norm_g, w_gate_up, w_down, *, tm=1024, tf=512):
    m = x.shape[0]
    nf = D_FF // tf
    return pl.pallas_call(
        _ffn_kernel,
        grid=(m // tm, nf),
        in_specs=[
            pl.BlockSpec((tm, D_MODEL), lambda i, j: (i, 0)),
            _const_spec((1, D_MODEL)),
            pl.BlockSpec((D_MODEL, tf), lambda i, j: (0, j)),
            pl.BlockSpec((D_MODEL, tf), lambda i, j: (0, j + nf)),
            pl.BlockSpec((tf, D_MODEL), lambda i, j: (j, 0)),
        ],
        out_specs=pl.BlockSpec((tm, D_MODEL), lambda i, j: (i, 0)),
        out_shape=jax.ShapeDtypeStruct((m, D_MODEL), F32),
        scratch_shapes=[pltpu.VMEM((tm, D_MODEL), BF16)],
        compiler_params=pltpu.CompilerParams(
            dimension_semantics=("parallel", "arbitrary"),
            vmem_limit_bytes=VMEM_LIMIT),
        name="ffn",
    )(x, norm_g, w_gate_up, w_gate_up, w_down)


def _inproj_kernel(x_ref, g_ref, w_ref, ws_ref, o_ref, os_ref, h_ref):
    j = pl.program_id(1)

    @pl.when(j == 0)
    def _():
        x = x_ref[...]
        h = (x * _rms_scale(x) * g_ref[...]).astype(BF16)
        h_ref[...] = h
        os_ref[...] = jnp.dot(h, ws_ref[...], preferred_element_type=F32)

    o_ref[...] = jnp.dot(h_ref[...], w_ref[...], preferred_element_type=F32).astype(BF16)


def _inproj(x, norm_g, w_main, w_side, *, tm=1024, tn=512):
    m = x.shape[0]
    return pl.pallas_call(
        _inproj_kernel,
        grid=(m // tm, MAIN_WIDTH // tn),
        in_specs=[
            pl.BlockSpec((tm, D_MODEL), lambda i, j: (i, 0)),
            _const_spec((1, D_MODEL)),
            pl.BlockSpec((D_MODEL, tn), lambda i, j: (0, j)),
            _const_spec((D_MODEL, SIDE_WIDTH)),
        ],
        out_specs=[
            pl.BlockSpec((tm, tn), lambda i, j: (i, j)),
            pl.BlockSpec((tm, SIDE_WIDTH), lambda i, j: (i, 0)),
        ],
        out_shape=[
            jax.ShapeDtypeStruct((m, MAIN_WIDTH), BF16),
            jax.ShapeDtypeStruct((m, SIDE_WIDTH), F32),
        ],
        scratch_shapes=[pltpu.VMEM((tm, D_MODEL), BF16)],
        compiler_params=pltpu.CompilerParams(
            dimension_semantics=("parallel", "arbitrary"),
            vmem_limit_bytes=VMEM_LIMIT),
        name="inproj",
    )(x, norm_g, w_main, w_side)


def _memkv_kernel(m_ref, g_ref, w_ref, kg_ref, k_ref, v_ref):
    x = m_ref[...]
    h = (x * _rms_scale(x) * g_ref[...]).astype(BF16)
    kv = jnp.dot(h, w_ref[...], preferred_element_type=F32)
    for hd in range(MEM_HEADS):
        sl = slice(hd * MEM_HEAD_DIM, (hd + 1) * MEM_HEAD_DIM)
        kh = kv[:, sl]
        k_ref[:, sl] = (kh * _rms_scale(kh) * kg_ref[...]).astype(BF16)
    v_ref[...] = kv[:, MEM_WIDTH:].astype(BF16)


def _memkv(mem, norm_g, w_kv, k_gain, *, tm=512):
    m = mem.shape[0]
    return pl.pallas_call(
        _memkv_kernel,
        grid=(m // tm,),
        in_specs=[
            pl.BlockSpec((tm, D_MODEL), lambda i: (i, 0)),
            _const_spec((1, D_MODEL)),
            _const_spec((D_MODEL, 2 * MEM_WIDTH)),
            _const_spec((1, MEM_HEAD_DIM)),
        ],
        out_specs=[
            pl.BlockSpec((tm, MEM_WIDTH), lambda i: (i, 0)),
            pl.BlockSpec((tm, MEM_WIDTH), lambda i: (i, 0)),
        ],
        out_shape=[
            jax.ShapeDtypeStruct((m, MEM_WIDTH), BF16),
            jax.ShapeDtypeStruct((m, MEM_WIDTH), BF16),
        ],
        compiler_params=pltpu.CompilerParams(
            dimension_semantics=("parallel",), vmem_limit_bytes=VMEM_LIMIT),
        name="memkv",
    )(mem, norm_g, w_kv, k_gain)


CUM_CHUNK = 256
N_PARTS = 3
ONES_LANE = N_PARTS * FOX_HEADS


def _split3(x):
    hi = x.astype(BF16)
    r = x - hi.astype(F32)
    mid = r.astype(BF16)
    lo = (r - mid.astype(F32)).astype(BF16)
    return hi, mid, lo


def _forget_cumsum_kernel(f_ref, b_ref, o_ref):
    seq = f_ref.shape[0]
    row = lax.broadcasted_iota(jnp.int32, (CUM_CHUNK, CUM_CHUNK), 0)
    col = lax.broadcasted_iota(jnp.int32, (CUM_CHUNK, CUM_CHUNK), 1)
    tri = jnp.where(row >= col, 1.0, 0.0).astype(BF16)
    lane = lax.broadcasted_iota(jnp.int32, (1, LANES), 1)
    group = jnp.right_shift(lane, FOX_HEADS.bit_length() - 1)
    one = jnp.ones((CUM_CHUNK, LANES), BF16)
    carry = jnp.zeros((1, LANES), F32)
    for ci in range(seq // CUM_CHUNK):
        z = f_ref[ci * CUM_CHUNK:(ci + 1) * CUM_CHUNK, :] + b_ref[...]
        logf = jnp.minimum(z, 0.0) - jnp.log(1.0 + jnp.exp(-jnp.abs(z)))
        c = carry
        for part in _split3(logf):
            c = c + jnp.dot(tri, part, preferred_element_type=F32)
        carry = c[CUM_CHUNK - 1:CUM_CHUNK, :]
        hi, mid, lo = _split3(c)
        o_ref[ci * CUM_CHUNK:(ci + 1) * CUM_CHUNK, :] = jnp.where(
            lane == ONES_LANE, one, jnp.where(group == 0, hi, jnp.where(group == 1, mid, lo)))


def _forget_cumsum(side, b_pad, *, batch, seq):
    side3 = side.reshape(batch, seq, SIDE_WIDTH)
    return pl.pallas_call(
        _forget_cumsum_kernel,
        grid=(batch,),
        in_specs=[
            pl.BlockSpec((None, seq, LANES), lambda b: (b, 0, POOL_WIDTH // LANES)),
            _const_spec((1, LANES)),
        ],
        out_specs=pl.BlockSpec((None, seq, LANES), lambda b: (b, 0, 0)),
        out_shape=jax.ShapeDtypeStruct((batch, seq, LANES), BF16),
        compiler_params=pltpu.CompilerParams(
            dimension_semantics=("parallel",), vmem_limit_bytes=VMEM_LIMIT),
        name="forget_cumsum",
    )(side3, b_pad)


FOX_BLOCK = 256


def _fox_kernel(q_ref, k_ref, v_ref, c_ref, qg_ref, kg_ref, o_ref,
                qa_ref, qb_ref, kat_ref, kbt_ref, va_ref, vb_ref):
    seq = q_ref.shape[0]
    blk = FOX_BLOCK
    nblk = seq // blk
    pair = pl.program_id(1)
    lane = lax.broadcasted_iota(jnp.int32, (1, LANES), 1)
    head_a = lane < FOX_HEAD_DIM
    head_b = jnp.logical_not(head_a)

    r = lax.broadcasted_iota(jnp.int32, (4 * LANES, 2 * LANES), 0)
    c = lax.broadcasted_iota(jnp.int32, (4 * LANES, 2 * LANES), 1)
    dim_bits = FOX_HEAD_DIM.bit_length() - 1
    same_head = (jnp.right_shift(jnp.bitwise_and(r, 2 * LANES - 1), dim_bits)
                 == jnp.right_shift(c, dim_bits))
    avg = jnp.where(same_head, 1.0 / FOX_HEAD_DIM, 0.0).astype(BF16)

    r = lax.broadcasted_iota(jnp.int32, (LANES, 4 * LANES), 0)
    c = lax.broadcasted_iota(jnp.int32, (LANES, 4 * LANES), 1)
    operand = jnp.right_shift(c, LANES.bit_length() - 1)
    is_query = operand < 2
    second = jnp.bitwise_and(operand, 1) == 1
    feat_lane = jnp.bitwise_and(c, LANES - 1) - jnp.where(second, 0, FOX_HEAD_DIM)
    var_lane = feat_lane - jnp.where(is_query, 0, N_PARTS)
    one_lane = feat_lane - jnp.where(is_query, N_PARTS, 0)
    part = jnp.right_shift(r, FOX_HEADS.bit_length() - 1)
    part_row = ((part < N_PARTS) & (var_lane == part)
                & (jnp.bitwise_and(r, FOX_HEADS - 1) == 2 * pair + second.astype(jnp.int32)))
    one_row = (r == ONES_LANE) & (one_lane >= 0) & (one_lane < N_PARTS)
    sel = jnp.where(part_row, jnp.where(is_query, 1.0, -1.0),
                    jnp.where(one_row, 1.0, 0.0)).astype(BF16)

    q_gain = qg_ref[...] * FOX_HEAD_DIM ** -0.5
    k_gain = kg_ref[...]
    gains = ((jnp.where(head_a, q_gain, 0.0), jnp.where(head_b, q_gain, 0.0)),
             (jnp.where(head_a, k_gain, 0.0), jnp.where(head_b, k_gain, 0.0)))
    ones_a = jnp.broadcast_to(jnp.where(head_a, 1.0, 0.0).astype(BF16), (blk, LANES))
    ones_b = jnp.broadcast_to(jnp.where(head_b, 1.0, 0.0).astype(BF16), (blk, LANES))

    def prepare(rows):
        q = q_ref[rows, :].astype(F32)
        k = k_ref[rows, :].astype(F32)
        sq = jnp.concatenate([q * q, k * k], axis=1)
        sq_hi = sq.astype(BF16)
        sq_lo = (sq - sq_hi.astype(F32)).astype(BF16)
        ms = jnp.dot(jnp.concatenate([sq_hi, sq_lo], axis=1), avg,
                     preferred_element_type=F32)
        extra = jnp.dot(c_ref[rows, :], sel, preferred_element_type=F32)
        qn = q * lax.rsqrt(ms[:, :LANES] + EPS)
        kn = k * lax.rsqrt(ms[:, LANES:] + EPS)
        for hd, (q_aug, kt_aug) in enumerate(((qa_ref, kat_ref), (qb_ref, kbt_ref))):
            q_aug[rows, :] = (qn * gains[0][hd]
                              + extra[:, hd * LANES:(hd + 1) * LANES]).astype(BF16)
            k_aug = kn * gains[1][hd] + extra[:, (2 + hd) * LANES:(3 + hd) * LANES]
            kt_aug[:, rows] = k_aug.T.astype(BF16)
        v = v_ref[rows, :]
        zero = jnp.zeros_like(v)
        va_ref[rows, 0:LANES] = jnp.where(head_a, v, zero)
        vb_ref[rows, 0:LANES] = jnp.where(head_b, v, zero)
        va_ref[rows, LANES:] = ones_a
        vb_ref[rows, LANES:] = ones_b

    row = lax.broadcasted_iota(jnp.int32, (blk, blk), 0)
    col = lax.broadcasted_iota(jnp.int32, (blk, blk), 1)
    causal = row >= col

    prepare(slice(0, blk))
    for qi in range(nblk):
        q_rows = slice(qi * blk, (qi + 1) * blk)
        past = slice(0, qi * blk)
        if qi + 1 < nblk:
            prepare(slice((qi + 1) * blk, (qi + 2) * blk))
        acc = None
        for q_aug, kt_aug, v_aug in ((qa_ref, kat_ref, va_ref), (qb_ref, kbt_ref, vb_ref)):
            q = q_aug[q_rows, :]
            s_diag = jnp.dot(q, kt_aug[:, q_rows], preferred_element_type=F32)
            s_diag = jnp.where(causal, s_diag, -jnp.inf)
            m = jnp.max(s_diag, axis=-1, keepdims=True)
            if qi > 0:
                s_past = jnp.dot(q, kt_aug[:, past], preferred_element_type=F32)
                m = jnp.maximum(m, jnp.max(s_past, axis=-1, keepdims=True))
                p_past = jnp.exp(s_past - m).astype(BF16)
                part = jnp.dot(p_past, v_aug[past, :], preferred_element_type=F32)
                acc = part if acc is None else acc + part
            p_diag = jnp.exp(s_diag - m).astype(BF16)
            part = jnp.dot(p_diag, v_aug[q_rows, :], preferred_element_type=F32)
            acc = part if acc is None else acc + part
        o_ref[q_rows, :] = (acc[:, :LANES] / acc[:, LANES:]).astype(BF16)


def _fox_attention(main3, c_all, q_gain2, k_gain2):
    batch, seq, _ = main3.shape
    qc, kc, vc = Q_COL // LANES, K_COL // LANES, V_COL // LANES
    tile = pltpu.VMEM((seq, LANES), BF16)
    tile_t = pltpu.VMEM((LANES, seq), BF16)
    wide = pltpu.VMEM((seq, 2 * LANES), BF16)
    return pl.pallas_call(
        _fox_kernel,
        grid=(batch, HEAD_PAIRS),
        in_specs=[
            pl.BlockSpec((None, seq, LANES), lambda b, p: (b, 0, qc + p)),
            pl.BlockSpec((None, seq, LANES), lambda b, p: (b, 0, kc + p)),
            pl.BlockSpec((None, seq, LANES), lambda b, p: (b, 0, vc + p)),
            pl.BlockSpec((None, seq, LANES), lambda b, p: (b, 0, 0)),
            _const_spec((1, LANES)),
            _const_spec((1, LANES)),
        ],
        out_specs=pl.BlockSpec((None, seq, LANES), lambda b, p: (b, 0, p)),
        out_shape=jax.ShapeDtypeStruct((batch, seq, FOX_WIDTH), BF16),
        scratch_shapes=[tile, tile, tile_t, tile_t, wide, wide],
        compiler_params=pltpu.CompilerParams(
            dimension_semantics=("parallel", "parallel"),
            vmem_limit_bytes=VMEM_LIMIT),
        name="fox_attention",
    )(main3, main3, main3, c_all, q_gain2, k_gain2)


def _mix_kernel(seq, x_ref, u_ref, halo_ref, gp_ref, gf_ref, gm_ref, qm_ref,
                km_ref, vm_ref, of_ref, pw_ref, ps_ref, wpu_ref, wfo_ref,
                wmo_ref, wo_ref, qg_ref, o_ref, ubuf_ref):
    tm = x_ref.shape[0]
    pos0 = (pl.program_id(0) * tm) % seq

    u = u_ref[...]
    halo_valid = (pos0 > 0).astype(F32)
    ubuf_ref[0:POOL_HALO, :] = halo_ref[...] * halo_valid
    ubuf_ref[POOL_HALO:, :] = u
    pos = pos0 + lax.broadcasted_iota(jnp.int32, (tm, 1), 0)
    mixed = []
    for g, win in enumerate(POOL_WINDOWS):
        sl = slice(g * POOL_GROUP_DIM, (g + 1) * POOL_GROUP_DIM)
        tot = u[:, sl]
        for d in range(1, win):
            tot = tot + ubuf_ref[POOL_HALO - d:POOL_HALO - d + tm, sl]
        count = jnp.minimum(pos + 1, win).astype(F32)
        diff = tot / count - u[:, sl]
        mg = jnp.dot(diff.astype(BF16), pw_ref[g], preferred_element_type=F32)
        mixed.append((mg * ps_ref[:, sl]).astype(BF16))
    y_pool = jnp.dot(jnp.concatenate(mixed, axis=-1), wpu_ref[...],
                     preferred_element_type=F32)

    nt = (((1,), (1,)), ((), ()))
    heads = []
    for hd in range(MEM_HEADS):
        sl = slice(hd * MEM_HEAD_DIM, (hd + 1) * MEM_HEAD_DIM)
        q = qm_ref[:, sl].astype(F32)
        qn = (q * _rms_scale(q) * qg_ref[...]).astype(BF16)
        s = lax.dot_general(qn, km_ref[:, sl], nt, preferred_element_type=F32)
        s = s * (MEM_HEAD_DIM ** -0.5)
        p = jnp.exp(s - jnp.max(s, axis=-1, keepdims=True))
        p = p / jnp.sum(p, axis=-1, keepdims=True)
        heads.append(jnp.dot(p.astype(BF16), vm_ref[:, sl],
                             preferred_element_type=F32).astype(BF16))
    y_mem = jnp.dot(jnp.concatenate(heads, axis=-1), wmo_ref[...],
                    preferred_element_type=F32)

    y_fox = jnp.dot(of_ref[...], wfo_ref[...], preferred_element_type=F32)
    merged = (jax.nn.sigmoid(gp_ref[...].astype(F32)) * y_pool
              + jax.nn.sigmoid(gf_ref[...].astype(F32)) * y_fox
              + jax.nn.sigmoid(gm_ref[...].astype(F32)) * y_mem)
    o_ref[...] = x_ref[...] + jnp.dot(merged.astype(BF16), wo_ref[...],
                                      preferred_element_type=F32)


def _mix(x, side, main, km, vm, o_fox, pool_w, pool_scale, w_pool_up, w_fox_o,
         w_mem_o, w_out, q_gain, *, seq, tm=256):
    m = x.shape[0]
    mem_len = km.shape[1]
    halo_blocks = tm // POOL_HALO
    gate_blk = D_MODEL
    return pl.pallas_call(
        functools.partial(_mix_kernel, seq),
        grid=(m // tm,),
        in_specs=[
            pl.BlockSpec((tm, D_MODEL), lambda i: (i, 0)),
            pl.BlockSpec((tm, POOL_WIDTH), lambda i: (i, 0)),
            pl.BlockSpec((POOL_HALO, POOL_WIDTH),
                         lambda i: (jnp.maximum(i * halo_blocks - 1, 0), 0)),
            pl.BlockSpec((tm, gate_blk), lambda i: (i, 0)),
            pl.BlockSpec((tm, gate_blk), lambda i: (i, 1)),
            pl.BlockSpec((tm, gate_blk), lambda i: (i, 2)),
            pl.BlockSpec((tm, MEM_WIDTH), lambda i: (i, QM_COL // MEM_WIDTH)),
            pl.BlockSpec((None, mem_len, MEM_WIDTH), lambda i: ((i * tm) // seq, 0, 0)),
            pl.BlockSpec((None, mem_len, MEM_WIDTH), lambda i: ((i * tm) // seq, 0, 0)),
            pl.BlockSpec((tm, FOX_WIDTH), lambda i: (i, 0)),
            _const_spec((POOL_GROUPS, POOL_GROUP_DIM, POOL_GROUP_DIM)),
            _const_spec((1, POOL_WIDTH)),
            _const_spec((POOL_WIDTH, D_MODEL)),
            _const_spec((FOX_WIDTH, D_MODEL)),
            _const_spec((MEM_WIDTH, D_MODEL)),
            _const_spec((D_MODEL, D_MODEL)),
            _const_spec((1, MEM_HEAD_DIM)),
        ],
        out_specs=pl.BlockSpec((tm, D_MODEL), lambda i: (i, 0)),
        out_shape=jax.ShapeDtypeStruct((m, D_MODEL), F32),
        scratch_shapes=[pltpu.VMEM((tm + POOL_HALO, POOL_WIDTH), F32)],
        compiler_params=pltpu.CompilerParams(
            dimension_semantics=("parallel",), vmem_limit_bytes=VMEM_LIMIT),
        name="mix",
    )(x, side, side, main, main, main, main, km, vm, o_fox, pool_w, pool_scale,
      w_pool_up, w_fox_o, w_mem_o, w_out, q_gain)


def kernel(x, mem, ffn1_norm, ffn1_w_gate_up, ffn1_w_down, mix_norm, mem_norm, w_in,
           b_forget, pool_w, pool_scale, w_pool_up, fox_q_norm, fox_k_norm, w_fox_o,
           w_mem_kv, mem_q_norm, mem_k_norm, w_mem_o, w_out,
           ffn2_norm, ffn2_w_gate_up, ffn2_w_down):
    batch, seq, _ = x.shape
    mem_len = mem.shape[1]
    depth = w_in.shape[0]
    xf = x.reshape(batch * seq, D_MODEL)
    memf = mem.reshape(batch * mem_len, D_MODEL)
    row = lambda v: v.reshape(1, -1).astype(F32)

    c_q = POOL_WIDTH
    c_f = POOL_WIDTH + 3 * FOX_WIDTH
    c_qm = c_f + FOX_HEADS
    c_gate = c_qm + MEM_WIDTH

    for l in range(depth):
        wi = w_in[l].astype(BF16)
        w_main = jnp.concatenate(
            [wi[:, c_gate:], wi[:, c_q:c_f], wi[:, c_qm:c_gate]], axis=1)
        w_f = wi[:, c_f:c_qm]
        w_side = jnp.pad(jnp.concatenate([wi[:, :POOL_WIDTH]] + [w_f] * N_PARTS, axis=1),
                         ((0, 0), (0, LANES - N_PARTS * FOX_HEADS)))
        b_pad = jnp.pad(jnp.tile(b_forget[l].astype(F32), N_PARTS),
                        (0, LANES - N_PARTS * FOX_HEADS)).reshape(1, LANES)

        xf = _ffn(xf, row(ffn1_norm[l]), ffn1_w_gate_up[l].astype(BF16),
                  ffn1_w_down[l].astype(BF16))

        main, side = _inproj(xf, row(mix_norm[l]), w_main, w_side)
        km, vm = _memkv(memf, row(mem_norm[l]), w_mem_kv[l].astype(BF16),
                        row(mem_k_norm[l]))

        c_all = _forget_cumsum(side, b_pad, batch=batch, seq=seq)
        o_fox = _fox_attention(main.reshape(batch, seq, MAIN_WIDTH), c_all,
                               row(jnp.tile(fox_q_norm[l], 2)),
                               row(jnp.tile(fox_k_norm[l], 2)))

        xf = _mix(xf, side, main, km.reshape(batch, mem_len, MEM_WIDTH),
                  vm.reshape(batch, mem_len, MEM_WIDTH),
                  o_fox.reshape(batch * seq, FOX_WIDTH),
                  pool_w[l].astype(BF16), row(pool_scale[l]), w_pool_up[l].astype(BF16),
                  w_fox_o[l].astype(BF16), w_mem_o[l].astype(BF16), w_out[l].astype(BF16),
                  row(mem_q_norm[l]), seq=seq)

        xf = _ffn(xf, row(ffn2_norm[l]), ffn2_w_gate_up[l].astype(BF16),
                  ffn2_w_down[l].astype(BF16))
    return xf.reshape(batch, seq, D_MODEL)
```

```python
import functools

import jax
import jax.numpy as jnp
from jax import lax
from jax.experimental import pallas as pl
from jax.experimental.pallas import tpu as pltpu

F32 = jnp.float32
BF16 = jnp.bfloat16

D_MODEL = 2048
D_FF = 5632
POOL_GROUPS = 4
POOL_GROUP_DIM = 128
POOL_WIDTH = POOL_GROUPS * POOL_GROUP_DIM
POOL_WINDOWS = (2, 4, 8, 16)
POOL_HALO = 16
FOX_HEADS = 16
FOX_HEAD_DIM = 64
FOX_WIDTH = FOX_HEADS * FOX_HEAD_DIM
MEM_HEADS = 4
MEM_HEAD_DIM = 128
MEM_WIDTH = MEM_HEADS * MEM_HEAD_DIM
GATE_WIDTH = 3 * D_MODEL
EPS = 1e-6

LANES = 128
HEAD_PAIRS = FOX_WIDTH // LANES

MAIN_WIDTH = GATE_WIDTH + 3 * FOX_WIDTH
Q_COL, K_COL, V_COL = GATE_WIDTH, GATE_WIDTH + FOX_WIDTH, GATE_WIDTH + 2 * FOX_WIDTH
SIDE_WIDTH = POOL_WIDTH + LANES

VMEM_LIMIT = 56 * 1024 * 1024


def _rms_scale(x):
    return lax.rsqrt(jnp.mean(x * x, axis=-1, keepdims=True) + EPS)


def _const_spec(shape):
    n = len(shape)
    return pl.BlockSpec(shape, lambda *_: (0,) * n, pipeline_mode=pl.Buffered(1))


def _ffn_kernel(x_ref, g_ref, wg_ref, wu_ref, wd_ref, o_ref, h_ref):
    j = pl.program_id(1)

    @pl.when(j == 0)
    def _():
        x = x_ref[...]
        h_ref[...] = (x * _rms_scale(x) * g_ref[...]).astype(BF16)
        o_ref[...] = x

    h = h_ref[...]
    gate = jnp.dot(h, wg_ref[...], preferred_element_type=F32)
    up = jnp.dot(h, wu_ref[...], preferred_element_type=F32)
    act = (0.5 * gate * jax.nn.sigmoid(gate) * up).astype(BF16)
    o_ref[...] += jnp.dot(act, wd_ref[...], preferred_element_type=F32)


def _ffn(x, norm_g, w_gate_up, w_down, *, tm=1024, tf=512):
    m = x.shape[0]
    nf = D_FF // tf
    return pl.pallas_call(
        _ffn_kernel,
        grid=(m // tm, nf),
        in_specs=[
            pl.BlockSpec((tm, D_MODEL), lambda i, j: (i, 0)),
            _const_spec((1, D_MODEL)),
            pl.BlockSpec((D_MODEL, tf), lambda i, j: (0, j)),
            pl.BlockSpec((D_MODEL, tf), lambda i, j: (0, j + nf)),
            pl.BlockSpec((tf, D_MODEL), lambda i, j: (j, 0)),
        ],
        out_specs=pl.BlockSpec((tm, D_MODEL), lambda i, j: (i, 0)),
        out_shape=jax.ShapeDtypeStruct((m, D_MODEL), F32),
        scratch_shapes=[pltpu.VMEM((tm, D_MODEL), BF16)],
        compiler_params=pltpu.CompilerParams(
            dimension_semantics=("parallel", "arbitrary"),
            vmem_limit_bytes=VMEM_LIMIT),
        name="ffn",
    )(x, norm_g, w_gate_up, w_gate_up, w_down)


def _inproj_kernel(x_ref, g_ref, w_ref, ws_ref, wq_ref, o_ref, os_ref, oq_ref, h_ref):
    j = pl.program_id(1)

    @pl.when(j == 0)
    def _():
        x = x_ref[...]
        h = (x * _rms_scale(x) * g_ref[...]).astype(BF16)
        h_ref[...] = h
        os_ref[...] = jnp.dot(h, ws_ref[...], preferred_element_type=F32)
        oq_ref[...] = jnp.dot(h, wq_ref[...], preferred_element_type=F32).astype(BF16)

    o_ref[...] = jnp.dot(h_ref[...], w_ref[...], preferred_element_type=F32).astype(BF16)


def _inproj(x, norm_g, w_main, w_side, w_qm, *, tm=1024, tn=1024):
    m = x.shape[0]
    return pl.pallas_call(
        _inproj_kernel,
        grid=(m // tm, MAIN_WIDTH // tn),
        in_specs=[
            pl.BlockSpec((tm, D_MODEL), lambda i, j: (i, 0)),
            _const_spec((1, D_MODEL)),
            pl.BlockSpec((D_MODEL, tn), lambda i, j: (0, j)),
            _const_spec((D_MODEL, SIDE_WIDTH)),
            _const_spec((D_MODEL, MEM_WIDTH)),
        ],
        out_specs=[
            pl.BlockSpec((tm, tn), lambda i, j: (i, j)),
            pl.BlockSpec((tm, SIDE_WIDTH), lambda i, j: (i, 0)),
            pl.BlockSpec((tm, MEM_WIDTH), lambda i, j: (i, 0)),
        ],
        out_shape=[
            jax.ShapeDtypeStruct((m, MAIN_WIDTH), BF16),
            jax.ShapeDtypeStruct((m, SIDE_WIDTH), F32),
            jax.ShapeDtypeStruct((m, MEM_WIDTH), BF16),
        ],
        scratch_shapes=[pltpu.VMEM((tm, D_MODEL), BF16)],
        compiler_params=pltpu.CompilerParams(
            dimension_semantics=("parallel", "arbitrary"),
            vmem_limit_bytes=VMEM_LIMIT),
        name="inproj",
    )(x, norm_g, w_main, w_side, w_qm)


def _memkv_kernel(m_ref, g_ref, w_ref, kg_ref, k_ref, v_ref):
    x = m_ref[...]
    h = (x * _rms_scale(x) * g_ref[...]).astype(BF16)
    kv = jnp.dot(h, w_ref[...], preferred_element_type=F32)
    for hd in range(MEM_HEADS):
        sl = slice(hd * MEM_HEAD_DIM, (hd + 1) * MEM_HEAD_DIM)
        kh = kv[:, sl]
        k_ref[:, sl] = (kh * _rms_scale(kh) * kg_ref[...]).astype(BF16)
    v_ref[...] = kv[:, MEM_WIDTH:].astype(BF16)


def _memkv(mem, norm_g, w_kv, k_gain, *, tm=512):
    m = mem.shape[0]
    return pl.pallas_call(
        _memkv_kernel,
        grid=(m // tm,),
        in_specs=[
            pl.BlockSpec((tm, D_MODEL), lambda i: (i, 0)),
            _const_spec((1, D_MODEL)),
            _const_spec((D_MODEL, 2 * MEM_WIDTH)),
            _const_spec((1, MEM_HEAD_DIM)),
        ],
        out_specs=[
            pl.BlockSpec((tm, MEM_WIDTH), lambda i: (i, 0)),
            pl.BlockSpec((tm, MEM_WIDTH), lambda i: (i, 0)),
        ],
        out_shape=[
            jax.ShapeDtypeStruct((m, MEM_WIDTH), BF16),
            jax.ShapeDtypeStruct((m, MEM_WIDTH), BF16),
        ],
        compiler_params=pltpu.CompilerParams(
            dimension_semantics=("parallel",), vmem_limit_bytes=VMEM_LIMIT),
        name="memkv",
    )(mem, norm_g, w_kv, k_gain)


CUM_CHUNK = 256
N_PARTS = 3
ONES_LANE = N_PARTS * FOX_HEADS


def _split3(x):
    hi = x.astype(BF16)
    r = x - hi.astype(F32)
    mid = r.astype(BF16)
    lo = (r - mid.astype(F32)).astype(BF16)
    return hi, mid, lo


def _forget_cumsum_kernel(f_ref, b_ref, o_ref):
    seq = f_ref.shape[0]
    row = lax.broadcasted_iota(jnp.int32, (CUM_CHUNK, CUM_CHUNK), 0)
    col = lax.broadcasted_iota(jnp.int32, (CUM_CHUNK, CUM_CHUNK), 1)
    tri = jnp.where(row >= col, 1.0, 0.0).astype(BF16)
    lane = lax.broadcasted_iota(jnp.int32, (1, LANES), 1)
    group = jnp.right_shift(lane, FOX_HEADS.bit_length() - 1)
    one = jnp.ones((CUM_CHUNK, LANES), BF16)
    carry = jnp.zeros((1, LANES), F32)
    for ci in range(seq // CUM_CHUNK):
        z = f_ref[ci * CUM_CHUNK:(ci + 1) * CUM_CHUNK, :] + b_ref[...]
        logf = jnp.minimum(z, 0.0) - jnp.log(1.0 + jnp.exp(-jnp.abs(z)))
        c = carry
        for part in _split3(logf):
            c = c + jnp.dot(tri, part, preferred_element_type=F32)
        carry = c[CUM_CHUNK - 1:CUM_CHUNK, :]
        hi, mid, lo = _split3(c)
        o_ref[ci * CUM_CHUNK:(ci + 1) * CUM_CHUNK, :] = jnp.where(
            lane == ONES_LANE, one, jnp.where(group == 0, hi, jnp.where(group == 1, mid, lo)))


def _forget_cumsum(side, b_pad, *, batch, seq):
    side3 = side.reshape(batch, seq, SIDE_WIDTH)
    return pl.pallas_call(
        _forget_cumsum_kernel,
        grid=(batch,),
        in_specs=[
            pl.BlockSpec((None, seq, LANES), lambda b: (b, 0, POOL_WIDTH // LANES)),
            _const_spec((1, LANES)),
        ],
        out_specs=pl.BlockSpec((None, seq, LANES), lambda b: (b, 0, 0)),
        out_shape=jax.ShapeDtypeStruct((batch, seq, LANES), BF16),
        compiler_params=pltpu.CompilerParams(
            dimension_semantics=("parallel",), vmem_limit_bytes=VMEM_LIMIT),
        name="forget_cumsum",
    )(side3, b_pad)


FOX_BLOCK = 256


def _fox_kernel(q_ref, k_ref, v_ref, c_ref, qg_ref, kg_ref, o_ref,
                qa_ref, qb_ref, kat_ref, kbt_ref, va_ref, vb_ref):
    seq = q_ref.shape[0]
    blk = FOX_BLOCK
    nblk = seq // blk
    pair = pl.program_id(1)
    lane = lax.broadcasted_iota(jnp.int32, (1, LANES), 1)
    head_a = lane < FOX_HEAD_DIM
    head_b = jnp.logical_not(head_a)

    r = lax.broadcasted_iota(jnp.int32, (4 * LANES, 2 * LANES), 0)
    c = lax.broadcasted_iota(jnp.int32, (4 * LANES, 2 * LANES), 1)
    dim_bits = FOX_HEAD_DIM.bit_length() - 1
    same_head = (jnp.right_shift(jnp.bitwise_and(r, 2 * LANES - 1), dim_bits)
                 == jnp.right_shift(c, dim_bits))
    avg = jnp.where(same_head, 1.0 / FOX_HEAD_DIM, 0.0).astype(BF16)

    r = lax.broadcasted_iota(jnp.int32, (LANES, 4 * LANES), 0)
    c = lax.broadcasted_iota(jnp.int32, (LANES, 4 * LANES), 1)
    operand = jnp.right_shift(c, LANES.bit_length() - 1)
    is_query = operand < 2
    second = jnp.bitwise_and(operand, 1) == 1
    feat_lane = jnp.bitwise_and(c, LANES - 1) - jnp.where(second, 0, FOX_HEAD_DIM)
    var_lane = feat_lane - jnp.where(is_query, 0, N_PARTS)
    one_lane = feat_lane - jnp.where(is_query, N_PARTS, 0)
    part = jnp.right_shift(r, FOX_HEADS.bit_length() - 1)
    part_row = ((part < N_PARTS) & (var_lane == part)
                & (jnp.bitwise_and(r, FOX_HEADS - 1) == 2 * pair + second.astype(jnp.int32)))
    one_row = (r == ONES_LANE) & (one_lane >= 0) & (one_lane < N_PARTS)
    sel = jnp.where(part_row, jnp.where(is_query, 1.0, -1.0),
                    jnp.where(one_row, 1.0, 0.0)).astype(BF16)

    q_gain = qg_ref[...] * FOX_HEAD_DIM ** -0.5
    k_gain = kg_ref[...]
    gains = ((jnp.where(head_a, q_gain, 0.0), jnp.where(head_b, q_gain, 0.0)),
             (jnp.where(head_a, k_gain, 0.0), jnp.where(head_b, k_gain, 0.0)))
    ones_a = jnp.broadcast_to(jnp.where(head_a, 1.0, 0.0).astype(BF16), (blk, LANES))
    ones_b = jnp.broadcast_to(jnp.where(head_b, 1.0, 0.0).astype(BF16), (blk, LANES))

    def prepare(rows):
        q = q_ref[rows, :].astype(F32)
        k = k_ref[rows, :].astype(F32)
        sq = jnp.concatenate([q * q, k * k], axis=1)
        sq_hi = sq.astype(BF16)
        sq_lo = (sq - sq_hi.astype(F32)).astype(BF16)
        ms = jnp.dot(jnp.concatenate([sq_hi, sq_lo], axis=1), avg,
                     preferred_element_type=F32)
        extra = jnp.dot(c_ref[rows, :], sel, preferred_element_type=F32)
        qn = q * lax.rsqrt(ms[:, :LANES] + EPS)
        kn = k * lax.rsqrt(ms[:, LANES:] + EPS)
        for hd, (q_aug, kt_aug) in enumerate(((qa_ref, kat_ref), (qb_ref, kbt_ref))):
            q_aug[rows, :] = (qn * gains[0][hd]
                              + extra[:, hd * LANES:(hd + 1) * LANES]).astype(BF16)
            k_aug = kn * gains[1][hd] + extra[:, (2 + hd) * LANES:(3 + hd) * LANES]
            kt_aug[:, rows] = k_aug.T.astype(BF16)
        v = v_ref[rows, :]
        zero = jnp.zeros_like(v)
        va_ref[rows, 0:LANES] = jnp.where(head_a, v, zero)
        vb_ref[rows, 0:LANES] = jnp.where(head_b, v, zero)
        va_ref[rows, LANES:] = ones_a
        vb_ref[rows, LANES:] = ones_b

    row = lax.broadcasted_iota(jnp.int32, (blk, blk), 0)
    col = lax.broadcasted_iota(jnp.int32, (blk, blk), 1)
    causal = row >= col

    prepare(slice(0, blk))
    for qi in range(nblk):
        q_rows = slice(qi * blk, (qi + 1) * blk)
        past = slice(0, qi * blk)
        if qi + 1 < nblk:
            prepare(slice((qi + 1) * blk, (qi + 2) * blk))
        acc = None
        for q_aug, kt_aug, v_aug in ((qa_ref, kat_ref, va_ref), (qb_ref, kbt_ref, vb_ref)):
            q = q_aug[q_rows, :]
            s_diag = jnp.dot(q, kt_aug[:, q_rows], preferred_element_type=F32)
            s_diag = jnp.where(causal, s_diag, -jnp.inf)
            m = jnp.max(s_diag, axis=-1, keepdims=True)
            if qi > 0:
                s_past = jnp.dot(q, kt_aug[:, past], preferred_element_type=F32)
                m = jnp.maximum(m, jnp.max(s_past, axis=-1, keepdims=True))
                p_past = jnp.exp(s_past - m).astype(BF16)
                part = jnp.dot(p_past, v_aug[past, :], preferred_element_type=F32)
                acc = part if acc is None else acc + part
            p_diag = jnp.exp(s_diag - m).astype(BF16)
            part = jnp.dot(p_diag, v_aug[q_rows, :], preferred_element_type=F32)
            acc = part if acc is None else acc + part
        o_ref[q_rows, :] = (acc[:, :LANES] / acc[:, LANES:]).astype(BF16)


def _fox_attention(main3, c_all, q_gain2, k_gain2):
    batch, seq, _ = main3.shape
    qc, kc, vc = Q_COL // LANES, K_COL // LANES, V_COL // LANES
    tile = pltpu.VMEM((seq, LANES), BF16)
    tile_t = pltpu.VMEM((LANES, seq), BF16)
    wide = pltpu.VMEM((seq, 2 * LANES), BF16)
    return pl.pallas_call(
        _fox_kernel,
        grid=(batch, HEAD_PAIRS),
        in_specs=[
            pl.BlockSpec((None, seq, LANES), lambda b, p: (b, 0, qc + p)),
            pl.BlockSpec((None, seq, LANES), lambda b, p: (b, 0, kc + p)),
            pl.BlockSpec((None, seq, LANES), lambda b, p: (b, 0, vc + p)),
            pl.BlockSpec((None, seq, LANES), lambda b, p: (b, 0, 0)),
            _const_spec((1, LANES)),
            _const_spec((1, LANES)),
        ],
        out_specs=pl.BlockSpec((None, seq, LANES), lambda b, p: (b, 0, p)),
        out_shape=jax.ShapeDtypeStruct((batch, seq, FOX_WIDTH), BF16),
        scratch_shapes=[tile, tile, tile_t, tile_t, wide, wide],
        compiler_params=pltpu.CompilerParams(
            dimension_semantics=("parallel", "parallel"),
            vmem_limit_bytes=VMEM_LIMIT),
        name="fox_attention",
    )(main3, main3, main3, c_all, q_gain2, k_gain2)


MIX_SLAB = 512


def _mix_kernel(seq, u_ref, halo_ref, gp_ref, gf_ref, gm_ref, qm_ref,
                km_ref, vm_ref, of_ref, pw_ref, ps_ref, wpu_ref, wfo_ref,
                wmo_ref, qg_ref, o_ref, ubuf_ref):
    tm = u_ref.shape[0]
    pos0 = (pl.program_id(0) * tm) % seq

    n_slabs = D_MODEL // MIX_SLAB
    slab = lambda s: slice(s * MIX_SLAB, (s + 1) * MIX_SLAB)

    u = u_ref[...]
    halo_valid = (pos0 > 0).astype(F32)
    ubuf_ref[0:POOL_HALO, :] = halo_ref[...] * halo_valid
    ubuf_ref[POOL_HALO:, :] = u
    pos = pos0 + lax.broadcasted_iota(jnp.int32, (tm, 1), 0)

    def pool_group(g):
        win = POOL_WINDOWS[g]
        sl = slice(g * POOL_GROUP_DIM, (g + 1) * POOL_GROUP_DIM)
        tot = u[:, sl]
        for d in range(1, win):
            tot = tot + ubuf_ref[POOL_HALO - d:POOL_HALO - d + tm, sl]
        count = jnp.minimum(pos + 1, win).astype(F32)
        diff = tot / count - u[:, sl]
        mg = jnp.dot(diff.astype(BF16), pw_ref[g], preferred_element_type=F32)
        return (mg * ps_ref[:, sl]).astype(BF16)

    nt = (((1,), (1,)), ((), ()))

    def mem_head(hd):
        sl = slice(hd * MEM_HEAD_DIM, (hd + 1) * MEM_HEAD_DIM)
        q = qm_ref[:, sl].astype(F32)
        qn = (q * _rms_scale(q) * qg_ref[...]).astype(BF16)
        s = lax.dot_general(qn, km_ref[:, sl], nt, preferred_element_type=F32)
        s = s * (MEM_HEAD_DIM ** -0.5)
        p = jnp.exp(s - jnp.max(s, axis=-1, keepdims=True))
        p = p / jnp.sum(p, axis=-1, keepdims=True)
        return jnp.dot(p.astype(BF16), vm_ref[:, sl],
                       preferred_element_type=F32).astype(BF16)

    mixed = jnp.concatenate([pool_group(g) for g in range(POOL_GROUPS)], axis=-1)
    heads = jnp.concatenate([mem_head(hd) for hd in range(MEM_HEADS)], axis=-1)
    o_fox = of_ref[...]

    def gate(ref, cols):
        return 0.5 * jnp.tanh(0.5 * ref[:, cols].astype(F32)) + 0.5

    for s in range(n_slabs):
        cols = slab(s)
        y_pool = jnp.dot(mixed, wpu_ref[:, cols], preferred_element_type=F32)
        y_fox = jnp.dot(o_fox, wfo_ref[:, cols], preferred_element_type=F32)
        y_mem = jnp.dot(heads, wmo_ref[:, cols], preferred_element_type=F32)
        merged = (gate(gp_ref, cols) * y_pool + gate(gf_ref, cols) * y_fox
                  + gate(gm_ref, cols) * y_mem)
        o_ref[:, cols] = merged.astype(BF16)


def _mix(side, main, q_m, km, vm, o_fox, pool_w, pool_scale, w_pool_up, w_fox_o,
         w_mem_o, q_gain, *, seq, tm=512):
    m = side.shape[0]
    mem_len = km.shape[1]
    halo_blocks = tm // POOL_HALO
    gate_blk = D_MODEL
    return pl.pallas_call(
        functools.partial(_mix_kernel, seq),
        grid=(m // tm,),
        in_specs=[
            pl.BlockSpec((tm, POOL_WIDTH), lambda i: (i, 0)),
            pl.BlockSpec((POOL_HALO, POOL_WIDTH),
                         lambda i: (jnp.maximum(i * halo_blocks - 1, 0), 0)),
            pl.BlockSpec((tm, gate_blk), lambda i: (i, 0)),
            pl.BlockSpec((tm, gate_blk), lambda i: (i, 1)),
            pl.BlockSpec((tm, gate_blk), lambda i: (i, 2)),
            pl.BlockSpec((tm, MEM_WIDTH), lambda i: (i, 0)),
            pl.BlockSpec((None, mem_len, MEM_WIDTH), lambda i: ((i * tm) // seq, 0, 0)),
            pl.BlockSpec((None, mem_len, MEM_WIDTH), lambda i: ((i * tm) // seq, 0, 0)),
            pl.BlockSpec((tm, FOX_WIDTH), lambda i: (i, 0)),
            _const_spec((POOL_GROUPS, POOL_GROUP_DIM, POOL_GROUP_DIM)),
            _const_spec((1, POOL_WIDTH)),
            _const_spec((POOL_WIDTH, D_MODEL)),
            _const_spec((FOX_WIDTH, D_MODEL)),
            _const_spec((MEM_WIDTH, D_MODEL)),
            _const_spec((1, MEM_HEAD_DIM)),
        ],
        out_specs=pl.BlockSpec((tm, D_MODEL), lambda i: (i, 0)),
        out_shape=jax.ShapeDtypeStruct((m, D_MODEL), BF16),
        scratch_shapes=[pltpu.VMEM((tm + POOL_HALO, POOL_WIDTH), F32)],
        compiler_params=pltpu.CompilerParams(
            dimension_semantics=("parallel",), vmem_limit_bytes=VMEM_LIMIT),
        name="mix",
    )(side, side, main, main, main, q_m, km, vm, o_fox, pool_w, pool_scale,
      w_pool_up, w_fox_o, w_mem_o, q_gain)


def _outproj_kernel(x_ref, m_ref, w_ref, o_ref):
    o_ref[...] = x_ref[...] + jnp.dot(m_ref[...], w_ref[...], preferred_element_type=F32)


def _outproj(x, merged, w_out, *, tm=512):
    m = x.shape[0]
    return pl.pallas_call(
        _outproj_kernel,
        grid=(m // tm,),
        in_specs=[
            pl.BlockSpec((tm, D_MODEL), lambda i: (i, 0)),
            pl.BlockSpec((tm, D_MODEL), lambda i: (i, 0)),
            _const_spec((D_MODEL, D_MODEL)),
        ],
        out_specs=pl.BlockSpec((tm, D_MODEL), lambda i: (i, 0)),
        out_shape=jax.ShapeDtypeStruct((m, D_MODEL), F32),
        compiler_params=pltpu.CompilerParams(
            dimension_semantics=("parallel",), vmem_limit_bytes=VMEM_LIMIT),
        name="outproj",
    )(x, merged, w_out)


def kernel(x, mem, ffn1_norm, ffn1_w_gate_up, ffn1_w_down, mix_norm, mem_norm, w_in,
           b_forget, pool_w, pool_scale, w_pool_up, fox_q_norm, fox_k_norm, w_fox_o,
           w_mem_kv, mem_q_norm, mem_k_norm, w_mem_o, w_out,
           ffn2_norm, ffn2_w_gate_up, ffn2_w_down):
    batch, seq, _ = x.shape
    mem_len = mem.shape[1]
    depth = w_in.shape[0]
    xf = x.reshape(batch * seq, D_MODEL)
    memf = mem.reshape(batch * mem_len, D_MODEL)
    row = lambda v: v.reshape(1, -1).astype(F32)

    c_q = POOL_WIDTH
    c_f = POOL_WIDTH + 3 * FOX_WIDTH
    c_qm = c_f + FOX_HEADS
    c_gate = c_qm + MEM_WIDTH

    for l in range(depth):
        wi = w_in[l].astype(BF16)
        w_main = jnp.concatenate([wi[:, c_gate:], wi[:, c_q:c_f]], axis=1)
        w_qm = wi[:, c_qm:c_gate]
        w_f = wi[:, c_f:c_qm]
        w_side = jnp.pad(jnp.concatenate([wi[:, :POOL_WIDTH]] + [w_f] * N_PARTS, axis=1),
                         ((0, 0), (0, LANES - N_PARTS * FOX_HEADS)))
        b_pad = jnp.pad(jnp.tile(b_forget[l].astype(F32), N_PARTS),
                        (0, LANES - N_PARTS * FOX_HEADS)).reshape(1, LANES)

        xf = _ffn(xf, row(ffn1_norm[l]), ffn1_w_gate_up[l].astype(BF16),
                  ffn1_w_down[l].astype(BF16))

        main, side, q_m = _inproj(xf, row(mix_norm[l]), w_main, w_side, w_qm)
        km, vm = _memkv(memf, row(mem_norm[l]), w_mem_kv[l].astype(BF16),
                        row(mem_k_norm[l]))

        c_all = _forget_cumsum(side, b_pad, batch=batch, seq=seq)
        o_fox = _fox_attention(main.reshape(batch, seq, MAIN_WIDTH), c_all,
                               row(jnp.tile(fox_q_norm[l], 2)),
                               row(jnp.tile(fox_k_norm[l], 2)))

        merged = _mix(side, main, q_m, km.reshape(batch, mem_len, MEM_WIDTH),
                      vm.reshape(batch, mem_len, MEM_WIDTH),
                      o_fox.reshape(batch * seq, FOX_WIDTH),
                      pool_w[l].astype(BF16), row(pool_scale[l]),
                      w_pool_up[l].astype(BF16), w_fox_o[l].astype(BF16),
                      w_mem_o[l].astype(BF16), row(mem_q_norm[l]), seq=seq)
        xf = _outproj(xf, merged, w_out[l].astype(BF16))

        xf = _ffn(xf, row(ffn2_norm[l]), ffn2_w_gate_up[l].astype(BF16),
                  ffn2_w_down[l].astype(BF16))
    return xf.reshape(batch, seq, D_MODEL)
```

```python
import functools

import jax
import jax.numpy as jnp
from jax import lax
from jax.experimental import pallas as pl
from jax.experimental.pallas import tpu as pltpu

F32 = jnp.float32
BF16 = jnp.bfloat16

D_MODEL = 2048
D_FF = 5632
POOL_GROUPS = 4
POOL_GROUP_DIM = 128
POOL_WIDTH = POOL_GROUPS * POOL_GROUP_DIM
POOL_WINDOWS = (2, 4, 8, 16)
POOL_HALO = 16
FOX_HEADS = 16
FOX_HEAD_DIM = 64
FOX_WIDTH = FOX_HEADS * FOX_HEAD_DIM
MEM_HEADS = 4
MEM_HEAD_DIM = 128
MEM_WIDTH = MEM_HEADS * MEM_HEAD_DIM
GATE_WIDTH = 3 * D_MODEL
EPS = 1e-6

LANES = 128
HEAD_PAIRS = FOX_WIDTH // LANES

MAIN_WIDTH = GATE_WIDTH + 3 * FOX_WIDTH
Q_COL, K_COL, V_COL = GATE_WIDTH, GATE_WIDTH + FOX_WIDTH, GATE_WIDTH + 2 * FOX_WIDTH
SIDE_WIDTH = POOL_WIDTH + LANES

VMEM_LIMIT = 56 * 1024 * 1024


def _rms_scale(x):
    return lax.rsqrt(jnp.mean(x * x, axis=-1, keepdims=True) + EPS)


def _const_spec(shape):
    n = len(shape)
    return pl.BlockSpec(shape, lambda *_: (0,) * n, pipeline_mode=pl.Buffered(1))


def _ffn_kernel(x_ref, g_ref, wg_ref, wu_ref, wd_ref, o_ref, h_ref):
    j = pl.program_id(1)

    @pl.when(j == 0)
    def _():
        x = x_ref[...]
        h_ref[...] = (x * _rms_scale(x) * g_ref[...]).astype(BF16)
        o_ref[...] = x

    h = h_ref[...]
    gate = jnp.dot(h, wg_ref[...], preferred_element_type=F32)
    up = jnp.dot(h, wu_ref[...], preferred_element_type=F32)
    act = (0.5 * gate * jax.nn.sigmoid(gate) * up).astype(BF16)
    o_ref[...] += jnp.dot(act, wd_ref[...], preferred_element_type=F32)


def _ffn(x, norm_g, w_gate_up, w_down, *, tm=1024, tf=512):
    m = x.shape[0]
    nf = D_FF // tf
    return pl.pallas_call(
        _ffn_kernel,
        grid=(m // tm, nf),
        in_specs=[
            pl.BlockSpec((tm, D_MODEL), lambda i, j: (i, 0)),
            _const_spec((1, D_MODEL)),
            pl.BlockSpec((D_MODEL, tf), lambda i, j: (0, j)),
            pl.BlockSpec((D_MODEL, tf), lambda i, j: (0, j + nf)),
            pl.BlockSpec((tf, D_MODEL), lambda i, j: (j, 0)),
        ],
        out_specs=pl.BlockSpec((tm, D_MODEL), lambda i, j: (i, 0)),
        out_shape=jax.ShapeDtypeStruct((m, D_MODEL), F32),
        scratch_shapes=[pltpu.VMEM((tm, D_MODEL), BF16)],
        compiler_params=pltpu.CompilerParams(
            dimension_semantics=("parallel", "arbitrary"),
            vmem_limit_bytes=VMEM_LIMIT),
        name="ffn",
    )(x, norm_g, w_gate_up, w_gate_up, w_down)


def _inproj_kernel(x_ref, g_ref, w_ref, ws_ref, wq_ref, o_ref, os_ref, oq_ref, h_ref):
    j = pl.program_id(1)

    @pl.when(j == 0)
    def _():
        x = x_ref[...]
        h_ref[...] = (x * _rms_scale(x) * g_ref[...]).astype(BF16)

    h = h_ref[...]
    o_ref[...] = jnp.dot(h, w_ref[...], preferred_element_type=F32).astype(BF16)

    @pl.when(j == pl.num_programs(1) - 1)
    def _():
        os_ref[...] = jnp.dot(h, ws_ref[...], preferred_element_type=F32)
        oq_ref[...] = jnp.dot(h, wq_ref[...], preferred_element_type=F32).astype(BF16)


def _inproj(x, norm_g, w_main, w_side, w_qm, *, tm=1024, tn=1024):
    m = x.shape[0]
    return pl.pallas_call(
        _inproj_kernel,
        grid=(m // tm, MAIN_WIDTH // tn),
        in_specs=[
            pl.BlockSpec((tm, D_MODEL), lambda i, j: (i, 0)),
            _const_spec((1, D_MODEL)),
            pl.BlockSpec((D_MODEL, tn), lambda i, j: (0, j)),
            _const_spec((D_MODEL, SIDE_WIDTH)),
            _const_spec((D_MODEL, MEM_WIDTH)),
        ],
        out_specs=[
            pl.BlockSpec((tm, tn), lambda i, j: (i, j)),
            pl.BlockSpec((tm, SIDE_WIDTH), lambda i, j: (i, 0)),
            pl.BlockSpec((tm, MEM_WIDTH), lambda i, j: (i, 0)),
        ],
        out_shape=[
            jax.ShapeDtypeStruct((m, MAIN_WIDTH), BF16),
            jax.ShapeDtypeStruct((m, SIDE_WIDTH), F32),
            jax.ShapeDtypeStruct((m, MEM_WIDTH), BF16),
        ],
        scratch_shapes=[pltpu.VMEM((tm, D_MODEL), BF16)],
        compiler_params=pltpu.CompilerParams(
            dimension_semantics=("parallel", "arbitrary"),
            vmem_limit_bytes=VMEM_LIMIT),
        name="inproj",
    )(x, norm_g, w_main, w_side, w_qm)


def _memkv_kernel(m_ref, g_ref, w_ref, kg_ref, k_ref, v_ref):
    x = m_ref[...]
    h = (x * _rms_scale(x) * g_ref[...]).astype(BF16)
    kv = jnp.dot(h, w_ref[...], preferred_element_type=F32)
    for hd in range(MEM_HEADS):
        sl = slice(hd * MEM_HEAD_DIM, (hd + 1) * MEM_HEAD_DIM)
        kh = kv[:, sl]
        k_ref[:, sl] = (kh * _rms_scale(kh) * kg_ref[...]).astype(BF16)
    v_ref[...] = kv[:, MEM_WIDTH:].astype(BF16)


def _memkv(mem, norm_g, w_kv, k_gain, *, tm=512):
    m = mem.shape[0]
    return pl.pallas_call(
        _memkv_kernel,
        grid=(m // tm,),
        in_specs=[
            pl.BlockSpec((tm, D_MODEL), lambda i: (i, 0)),
            _const_spec((1, D_MODEL)),
            _const_spec((D_MODEL, 2 * MEM_WIDTH)),
            _const_spec((1, MEM_HEAD_DIM)),
        ],
        out_specs=[
            pl.BlockSpec((tm, MEM_WIDTH), lambda i: (i, 0)),
            pl.BlockSpec((tm, MEM_WIDTH), lambda i: (i, 0)),
        ],
        out_shape=[
            jax.ShapeDtypeStruct((m, MEM_WIDTH), BF16),
            jax.ShapeDtypeStruct((m, MEM_WIDTH), BF16),
        ],
        compiler_params=pltpu.CompilerParams(
            dimension_semantics=("parallel",), vmem_limit_bytes=VMEM_LIMIT),
        name="memkv",
    )(mem, norm_g, w_kv, k_gain)


CUM_CHUNK = 256
N_PARTS = 3
ONES_LANE = N_PARTS * FOX_HEADS


def _split3(x):
    hi = x.astype(BF16)
    r = x - hi.astype(F32)
    mid = r.astype(BF16)
    lo = (r - mid.astype(F32)).astype(BF16)
    return hi, mid, lo


def _forget_cumsum_kernel(f_ref, b_ref, o_ref):
    seq = f_ref.shape[0]
    row = lax.broadcasted_iota(jnp.int32, (CUM_CHUNK, CUM_CHUNK), 0)
    col = lax.broadcasted_iota(jnp.int32, (CUM_CHUNK, CUM_CHUNK), 1)
    tri = jnp.where(row >= col, 1.0, 0.0).astype(BF16)
    lane = lax.broadcasted_iota(jnp.int32, (1, LANES), 1)
    group = jnp.right_shift(lane, FOX_HEADS.bit_length() - 1)
    one = jnp.ones((CUM_CHUNK, LANES), BF16)
    carry = jnp.zeros((1, LANES), F32)
    for ci in range(seq // CUM_CHUNK):
        z = f_ref[ci * CUM_CHUNK:(ci + 1) * CUM_CHUNK, :] + b_ref[...]
        logf = jnp.minimum(z, 0.0) - jnp.log(1.0 + jnp.exp(-jnp.abs(z)))
        c = carry
        for part in _split3(logf):
            c = c + jnp.dot(tri, part, preferred_element_type=F32)
        carry = c[CUM_CHUNK - 1:CUM_CHUNK, :]
        hi, mid, lo = _split3(c)
        o_ref[ci * CUM_CHUNK:(ci + 1) * CUM_CHUNK, :] = jnp.where(
            lane == ONES_LANE, one, jnp.where(group == 0, hi, jnp.where(group == 1, mid, lo)))


def _forget_cumsum(side, b_pad, *, batch, seq):
    side3 = side.reshape(batch, seq, SIDE_WIDTH)
    return pl.pallas_call(
        _forget_cumsum_kernel,
        grid=(batch,),
        in_specs=[
            pl.BlockSpec((None, seq, LANES), lambda b: (b, 0, POOL_WIDTH // LANES)),
            _const_spec((1, LANES)),
        ],
        out_specs=pl.BlockSpec((None, seq, LANES), lambda b: (b, 0, 0)),
        out_shape=jax.ShapeDtypeStruct((batch, seq, LANES), BF16),
        compiler_params=pltpu.CompilerParams(
            dimension_semantics=("parallel",), vmem_limit_bytes=VMEM_LIMIT),
        name="forget_cumsum",
    )(side3, b_pad)


FOX_BLOCK = 256


FOX_PAIRS_PER_STEP = 2
FOX_SCRATCH_PER_PAIR = 7


def _fox_kernel(q_ref, k_ref, v_ref, c_ref, qg_ref, kg_ref, o_ref, *scratch):
    for i in range(FOX_PAIRS_PER_STEP):
        lanes = slice(i * LANES, (i + 1) * LANES)
        _fox_pair(pl.program_id(1) * FOX_PAIRS_PER_STEP + i,
                  q_ref.at[:, lanes], k_ref.at[:, lanes], v_ref.at[:, lanes], c_ref,
                  qg_ref, kg_ref, o_ref.at[:, lanes],
                  *scratch[i * FOX_SCRATCH_PER_PAIR:(i + 1) * FOX_SCRATCH_PER_PAIR])


def _fox_pair(pair, q_ref, k_ref, v_ref, c_ref, qg_ref, kg_ref, o_ref,
              qa_ref, qb_ref, kat_ref, kbt_ref, va_ref, vb_ref, extra_ref):
    seq = q_ref.shape[0]
    blk = FOX_BLOCK
    nblk = seq // blk
    lane = lax.broadcasted_iota(jnp.int32, (1, LANES), 1)
    head_a = lane < FOX_HEAD_DIM
    head_b = jnp.logical_not(head_a)

    r = lax.broadcasted_iota(jnp.int32, (LANES, 4 * LANES), 0)
    c = lax.broadcasted_iota(jnp.int32, (LANES, 4 * LANES), 1)
    operand = jnp.right_shift(c, LANES.bit_length() - 1)
    is_query = operand < 2
    second = jnp.bitwise_and(operand, 1) == 1
    feat_lane = jnp.bitwise_and(c, LANES - 1) - jnp.where(second, 0, FOX_HEAD_DIM)
    var_lane = feat_lane - jnp.where(is_query, 0, N_PARTS)
    one_lane = feat_lane - jnp.where(is_query, N_PARTS, 0)
    part = jnp.right_shift(r, FOX_HEADS.bit_length() - 1)
    part_row = ((part < N_PARTS) & (var_lane == part)
                & (jnp.bitwise_and(r, FOX_HEADS - 1) == 2 * pair + second.astype(jnp.int32)))
    one_row = (r == ONES_LANE) & (one_lane >= 0) & (one_lane < N_PARTS)
    sel = jnp.where(part_row, jnp.where(is_query, 1.0, -1.0),
                    jnp.where(one_row, 1.0, 0.0)).astype(BF16)

    q_gain = qg_ref[...] * FOX_HEAD_DIM ** -0.5
    k_gain = kg_ref[...]
    gains = ((jnp.where(head_a, q_gain, 0.0), jnp.where(head_b, q_gain, 0.0)),
             (jnp.where(head_a, k_gain, 0.0), jnp.where(head_b, k_gain, 0.0)))
    ones_a = jnp.broadcast_to(jnp.where(head_a, 1.0, 0.0).astype(BF16), (blk, LANES))
    ones_b = jnp.broadcast_to(jnp.where(head_b, 1.0, 0.0).astype(BF16), (blk, LANES))

    extra_ref[...] = jnp.dot(c_ref[...], sel, preferred_element_type=F32).astype(BF16)

    def head_rms_scale(x):
        sq = x * x
        s_a = jnp.sum(jnp.where(head_a, sq, 0.0), axis=-1, keepdims=True)
        s_b = jnp.sum(jnp.where(head_a, 0.0, sq), axis=-1, keepdims=True)
        return lax.rsqrt(jnp.where(head_a, s_a, s_b) * (1.0 / FOX_HEAD_DIM) + EPS)

    def prepare(rows):
        q = q_ref[rows, :].astype(F32)
        k = k_ref[rows, :].astype(F32)
        extra = extra_ref[rows, :].astype(F32)
        qn = q * head_rms_scale(q)
        kn = k * head_rms_scale(k)
        for hd, (q_aug, kt_aug) in enumerate(((qa_ref, kat_ref), (qb_ref, kbt_ref))):
            q_aug[rows, :] = (qn * gains[0][hd]
                              + extra[:, hd * LANES:(hd + 1) * LANES]).astype(BF16)
            k_aug = kn * gains[1][hd] + extra[:, (2 + hd) * LANES:(3 + hd) * LANES]
            kt_aug[:, rows] = k_aug.T.astype(BF16)
        v = v_ref[rows, :]
        zero = jnp.zeros_like(v)
        va_ref[rows, 0:LANES] = jnp.where(head_a, v, zero)
        vb_ref[rows, 0:LANES] = jnp.where(head_b, v, zero)
        va_ref[rows, LANES:] = ones_a
        vb_ref[rows, LANES:] = ones_b

    row = lax.broadcasted_iota(jnp.int32, (blk, blk), 0)
    col = lax.broadcasted_iota(jnp.int32, (blk, blk), 1)
    causal = row >= col

    prepare(slice(0, blk))
    for qi in range(nblk):
        q_rows = slice(qi * blk, (qi + 1) * blk)
        past = slice(0, qi * blk)
        if qi + 1 < nblk:
            prepare(slice((qi + 1) * blk, (qi + 2) * blk))
        acc = None
        for q_aug, kt_aug, v_aug in ((qa_ref, kat_ref, va_ref), (qb_ref, kbt_ref, vb_ref)):
            q = q_aug[q_rows, :]
            s_diag = jnp.dot(q, kt_aug[:, q_rows], preferred_element_type=F32)
            s_diag = jnp.where(causal, s_diag, -jnp.inf)
            m = jnp.max(s_diag, axis=-1, keepdims=True)
            if qi > 0:
                s_past = jnp.dot(q, kt_aug[:, past], preferred_element_type=F32)
                m = jnp.maximum(m, jnp.max(s_past, axis=-1, keepdims=True))
                p_past = jnp.exp(s_past - m).astype(BF16)
                part = jnp.dot(p_past, v_aug[past, :], preferred_element_type=F32)
                acc = part if acc is None else acc + part
            p_diag = jnp.exp(s_diag - m).astype(BF16)
            part = jnp.dot(p_diag, v_aug[q_rows, :], preferred_element_type=F32)
            acc = part if acc is None else acc + part
        o_ref[q_rows, :] = (acc[:, :LANES] / acc[:, LANES:]).astype(BF16)


def _fox_attention(main3, c_all, q_gain2, k_gain2):
    batch, seq, _ = main3.shape
    tile = pltpu.VMEM((seq, LANES), BF16)
    tile_t = pltpu.VMEM((LANES, seq), BF16)
    wide = pltpu.VMEM((seq, 2 * LANES), BF16)
    width = FOX_PAIRS_PER_STEP * LANES
    qc, kc, vc = Q_COL // width, K_COL // width, V_COL // width
    per_pair = [tile, tile, tile_t, tile_t, wide, wide, pltpu.VMEM((seq, 4 * LANES), BF16)]
    assert len(per_pair) == FOX_SCRATCH_PER_PAIR
    return pl.pallas_call(
        _fox_kernel,
        grid=(batch, HEAD_PAIRS // FOX_PAIRS_PER_STEP),
        in_specs=[
            pl.BlockSpec((None, seq, width), lambda b, p: (b, 0, qc + p)),
            pl.BlockSpec((None, seq, width), lambda b, p: (b, 0, kc + p)),
            pl.BlockSpec((None, seq, width), lambda b, p: (b, 0, vc + p)),
            pl.BlockSpec((None, seq, LANES), lambda b, p: (b, 0, 0)),
            _const_spec((1, LANES)),
            _const_spec((1, LANES)),
        ],
        out_specs=pl.BlockSpec((None, seq, width), lambda b, p: (b, 0, p)),
        out_shape=jax.ShapeDtypeStruct((batch, seq, FOX_WIDTH), BF16),
        scratch_shapes=per_pair * FOX_PAIRS_PER_STEP,
        compiler_params=pltpu.CompilerParams(
            dimension_semantics=("parallel", "parallel"),
            vmem_limit_bytes=VMEM_LIMIT),
        name="fox_attention",
    )(main3, main3, main3, c_all, q_gain2, k_gain2)


MIX_SLAB = 512


def _mix_kernel(seq, u_ref, halo_ref, gp_ref, gf_ref, gm_ref, qm_ref,
                km_ref, vm_ref, of_ref, pw_ref, ps_ref, wpu_ref, wfo_ref,
                wmo_ref, qg_ref, o_ref, ubuf_ref):
    tm = u_ref.shape[0]
    pos0 = (pl.program_id(0) * tm) % seq

    n_slabs = D_MODEL // MIX_SLAB
    slab = lambda s: slice(s * MIX_SLAB, (s + 1) * MIX_SLAB)

    u = u_ref[...]
    halo_valid = (pos0 > 0).astype(F32)
    ubuf_ref[0:POOL_HALO, :] = halo_ref[...] * halo_valid
    ubuf_ref[POOL_HALO:, :] = u
    pos = pos0 + lax.broadcasted_iota(jnp.int32, (tm, 1), 0)

    def pool_group(g):
        win = POOL_WINDOWS[g]
        sl = slice(g * POOL_GROUP_DIM, (g + 1) * POOL_GROUP_DIM)
        tot = u[:, sl]
        for d in range(1, win):
            tot = tot + ubuf_ref[POOL_HALO - d:POOL_HALO - d + tm, sl]
        count = jnp.minimum(pos + 1, win).astype(F32)
        diff = tot / count - u[:, sl]
        mg = jnp.dot(diff.astype(BF16), pw_ref[g], preferred_element_type=F32)
        return (mg * ps_ref[:, sl]).astype(BF16)

    nt = (((1,), (1,)), ((), ()))

    def mem_head(hd):
        sl = slice(hd * MEM_HEAD_DIM, (hd + 1) * MEM_HEAD_DIM)
        q = qm_ref[:, sl].astype(F32)
        qn = (q * _rms_scale(q) * qg_ref[...]).astype(BF16)
        s = lax.dot_general(qn, km_ref[:, sl], nt, preferred_element_type=F32)
        s = s * (MEM_HEAD_DIM ** -0.5)
        p = jnp.exp(s - jnp.max(s, axis=-1, keepdims=True))
        p = p / jnp.sum(p, axis=-1, keepdims=True)
        return jnp.dot(p.astype(BF16), vm_ref[:, sl],
                       preferred_element_type=F32).astype(BF16)

    mixed = jnp.concatenate([pool_group(g) for g in range(POOL_GROUPS)], axis=-1)
    heads = jnp.concatenate([mem_head(hd) for hd in range(MEM_HEADS)], axis=-1)
    o_fox = of_ref[...]

    def gate(ref, cols):
        return 0.5 * jnp.tanh(0.5 * ref[:, cols].astype(F32)) + 0.5

    for s in range(n_slabs):
        cols = slab(s)
        y_pool = jnp.dot(mixed, wpu_ref[:, cols], preferred_element_type=F32)
        y_fox = jnp.dot(o_fox, wfo_ref[:, cols], preferred_element_type=F32)
        y_mem = jnp.dot(heads, wmo_ref[:, cols], preferred_element_type=F32)
        merged = (gate(gp_ref, cols) * y_pool + gate(gf_ref, cols) * y_fox
                  + gate(gm_ref, cols) * y_mem)
        o_ref[:, cols] = merged.astype(BF16)


def _mix(side, main, q_m, km, vm, o_fox, pool_w, pool_scale, w_pool_up, w_fox_o,
         w_mem_o, q_gain, *, seq, tm=512):
    m = side.shape[0]
    mem_len = km.shape[1]
    halo_blocks = tm // POOL_HALO
    gate_blk = D_MODEL
    return pl.pallas_call(
        functools.partial(_mix_kernel, seq),
        grid=(m // tm,),
        in_specs=[
            pl.BlockSpec((tm, POOL_WIDTH), lambda i: (i, 0)),
            pl.BlockSpec((POOL_HALO, POOL_WIDTH),
                         lambda i: (jnp.maximum(i * halo_blocks - 1, 0), 0)),
            pl.BlockSpec((tm, gate_blk), lambda i: (i, 0)),
            pl.BlockSpec((tm, gate_blk), lambda i: (i, 1)),
            pl.BlockSpec((tm, gate_blk), lambda i: (i, 2)),
            pl.BlockSpec((tm, MEM_WIDTH), lambda i: (i, 0)),
            pl.BlockSpec((None, mem_len, MEM_WIDTH), lambda i: ((i * tm) // seq, 0, 0)),
            pl.BlockSpec((None, mem_len, MEM_WIDTH), lambda i: ((i * tm) // seq, 0, 0)),
            pl.BlockSpec((tm, FOX_WIDTH), lambda i: (i, 0)),
            _const_spec((POOL_GROUPS, POOL_GROUP_DIM, POOL_GROUP_DIM)),
            _const_spec((1, POOL_WIDTH)),
            _const_spec((POOL_WIDTH, D_MODEL)),
            _const_spec((FOX_WIDTH, D_MODEL)),
            _const_spec((MEM_WIDTH, D_MODEL)),
            _const_spec((1, MEM_HEAD_DIM)),
        ],
        out_specs=pl.BlockSpec((tm, D_MODEL), lambda i: (i, 0)),
        out_shape=jax.ShapeDtypeStruct((m, D_MODEL), BF16),
        scratch_shapes=[pltpu.VMEM((tm + POOL_HALO, POOL_WIDTH), F32)],
        compiler_params=pltpu.CompilerParams(
            dimension_semantics=("parallel",), vmem_limit_bytes=VMEM_LIMIT),
        name="mix",
    )(side, side, main, main, main, q_m, km, vm, o_fox, pool_w, pool_scale,
      w_pool_up, w_fox_o, w_mem_o, q_gain)


def _outproj_kernel(x_ref, m_ref, w_ref, o_ref):
    o_ref[...] = x_ref[...] + jnp.dot(m_ref[...], w_ref[...], preferred_element_type=F32)


def _outproj(x, merged, w_out, *, tm=512):
    m = x.shape[0]
    return pl.pallas_call(
        _outproj_kernel,
        grid=(m // tm,),
        in_specs=[
            pl.BlockSpec((tm, D_MODEL), lambda i: (i, 0)),
            pl.BlockSpec((tm, D_MODEL), lambda i: (i, 0)),
            _const_spec((D_MODEL, D_MODEL)),
        ],
        out_specs=pl.BlockSpec((tm, D_MODEL), lambda i: (i, 0)),
        out_shape=jax.ShapeDtypeStruct((m, D_MODEL), F32),
        compiler_params=pltpu.CompilerParams(
            dimension_semantics=("parallel",), vmem_limit_bytes=VMEM_LIMIT),
        name="outproj",
    )(x, merged, w_out)


def kernel(x, mem, ffn1_norm, ffn1_w_gate_up, ffn1_w_down, mix_norm, mem_norm, w_in,
           b_forget, pool_w, pool_scale, w_pool_up, fox_q_norm, fox_k_norm, w_fox_o,
           w_mem_kv, mem_q_norm, mem_k_norm, w_mem_o, w_out,
           ffn2_norm, ffn2_w_gate_up, ffn2_w_down):
    batch, seq, _ = x.shape
    mem_len = mem.shape[1]
    depth = w_in.shape[0]
    xf = x.reshape(batch * seq, D_MODEL)
    memf = mem.reshape(batch * mem_len, D_MODEL)
    row = lambda v: v.reshape(1, -1).astype(F32)

    c_q = POOL_WIDTH
    c_f = POOL_WIDTH + 3 * FOX_WIDTH
    c_qm = c_f + FOX_HEADS
    c_gate = c_qm + MEM_WIDTH

    for l in range(depth):
        wi = w_in[l].astype(BF16)
        w_main = jnp.concatenate([wi[:, c_gate:], wi[:, c_q:c_f]], axis=1)
        w_qm = wi[:, c_qm:c_gate]
        w_f = wi[:, c_f:c_qm]
        w_side = jnp.pad(jnp.concatenate([wi[:, :POOL_WIDTH]] + [w_f] * N_PARTS, axis=1),
                         ((0, 0), (0, LANES - N_PARTS * FOX_HEADS)))
        b_pad = jnp.pad(jnp.tile(b_forget[l].astype(F32), N_PARTS),
                        (0, LANES - N_PARTS * FOX_HEADS)).reshape(1, LANES)

        xf = _ffn(xf, row(ffn1_norm[l]), ffn1_w_gate_up[l].astype(BF16),
                  ffn1_w_down[l].astype(BF16))

        main, side, q_m = _inproj(xf, row(mix_norm[l]), w_main, w_side, w_qm)
        km, vm = _memkv(memf, row(mem_norm[l]), w_mem_kv[l].astype(BF16),
                        row(mem_k_norm[l]))

        c_all = _forget_cumsum(side, b_pad, batch=batch, seq=seq)
        o_fox = _fox_attention(main.reshape(batch, seq, MAIN_WIDTH), c_all,
                               row(jnp.tile(fox_q_norm[l], 2)),
                               row(jnp.tile(fox_k_norm[l], 2)))

        merged = _mix(side, main, q_m, km.reshape(batch, mem_len, MEM_WIDTH),
                      vm.reshape(batch, mem_len, MEM_WIDTH),
                      o_fox.reshape(batch * seq, FOX_WIDTH),
                      pool_w[l].astype(BF16), row(pool_scale[l]),
                      w_pool_up[l].astype(BF16), w_fox_o[l].astype(BF16),
                      w_mem_o[l].astype(BF16), row(mem_q_norm[l]), seq=seq)
        xf = _outproj(xf, merged, w_out[l].astype(BF16))

        xf = _ffn(xf, row(ffn2_norm[l]), ffn2_w_gate_up[l].astype(BF16),
                  ffn2_w_down[l].astype(BF16))
    return xf.reshape(batch, seq, D_MODEL)
```

```python
import functools

import jax
import jax.numpy as jnp
from jax import lax
from jax.experimental import pallas as pl
from jax.experimental.pallas import tpu as pltpu

F32 = jnp.float32
BF16 = jnp.bfloat16

D_MODEL = 2048
D_FF = 5632
POOL_GROUPS = 4
POOL_GROUP_DIM = 128
POOL_WIDTH = POOL_GROUPS * POOL_GROUP_DIM
POOL_WINDOWS = (2, 4, 8, 16)
POOL_HALO = 16
FOX_HEADS = 16
FOX_HEAD_DIM = 64
FOX_WIDTH = FOX_HEADS * FOX_HEAD_DIM
MEM_HEADS = 4
MEM_HEAD_DIM = 128
MEM_WIDTH = MEM_HEADS * MEM_HEAD_DIM
GATE_WIDTH = 3 * D_MODEL
EPS = 1e-6

LANES = 128
HEAD_PAIRS = FOX_WIDTH // LANES

MAIN_WIDTH = GATE_WIDTH + 3 * FOX_WIDTH
Q_COL, K_COL, V_COL = GATE_WIDTH, GATE_WIDTH + FOX_WIDTH, GATE_WIDTH + 2 * FOX_WIDTH
SIDE_WIDTH = POOL_WIDTH + LANES

VMEM_LIMIT = 56 * 1024 * 1024
FFN_VMEM_LIMIT = 58 * 1024 * 1024


def _rms_scale(x):
    return lax.rsqrt(jnp.mean(x * x, axis=-1, keepdims=True) + EPS)


def _const_spec(shape):
    n = len(shape)
    return pl.BlockSpec(shape, lambda *_: (0,) * n, pipeline_mode=pl.Buffered(1))


BF16_SUBLANES = 16


def _ffn_kernel(n_cast, split_w_in, x_ref, g_ref, wg_ref, wu_ref, wd_ref, *refs):
    cast_in, o_ref, cast_out, h_ref = (refs[:n_cast], refs[n_cast],
                                       refs[n_cast + 1:2 * n_cast + 1], refs[-1])
    if split_w_in:
        w_in_ref = refs[n_cast]
        o_ref = refs[n_cast + 1]
        cast_out = refs[n_cast + 2:2 * n_cast + 2]
        w_main_ref, w_side_ref, w_qm_ref = refs[2 * n_cast + 2:2 * n_cast + 5]
    j = pl.program_id(1)

    @pl.when(j == 0)
    def _():
        x = x_ref[...]
        h_ref[...] = (x * _rms_scale(x) * g_ref[...]).astype(BF16)
        o_ref[...] = x

    h = h_ref[...]
    gate = jnp.dot(h, wg_ref[...], preferred_element_type=F32)
    for src, dst in zip(cast_in, cast_out):
        dst[...] = src[...].astype(BF16)
    up = jnp.dot(h, wu_ref[...], preferred_element_type=F32)
    act = (0.5 * gate * jax.nn.sigmoid(gate) * up).astype(BF16)
    o_ref[...] += jnp.dot(act, wd_ref[...], preferred_element_type=F32)

    if split_w_in:
        step = pl.program_id(0) * pl.num_programs(1) + j

        @pl.when(step < D_MODEL // BF16_SUBLANES)
        def _():
            w = w_in_ref[...]
            c_q, c_f = POOL_WIDTH, POOL_WIDTH + 3 * FOX_WIDTH
            c_qm = c_f + FOX_HEADS
            c_gate = c_qm + MEM_WIDTH
            w_main_ref[...] = jnp.concatenate(
                [w[:, c_gate:], w[:, c_q:c_f]], axis=1).astype(BF16)
            w_qm_ref[...] = w[:, c_qm:c_gate].astype(BF16)
            w_f = w[:, c_f:c_qm]
            pad = jnp.zeros((w.shape[0], LANES - N_PARTS * FOX_HEADS), F32)
            w_side_ref[...] = jnp.concatenate(
                [w[:, :POOL_WIDTH]] + [w_f] * N_PARTS + [pad], axis=1).astype(BF16)


def _cast_block_spec(shape, ni, nj):
    rows, cols = shape
    steps = ni * nj
    if rows % (steps * BF16_SUBLANES) == 0:
        return pl.BlockSpec((rows // steps, cols), lambda i, j: (i * nj + j, 0))
    if (rows % (ni * BF16_SUBLANES) == 0 and cols % (nj * LANES) == 0):
        return pl.BlockSpec((rows // ni, cols // nj), lambda i, j: (i, j))
    assert rows % (ni * BF16_SUBLANES) == 0, shape
    return pl.BlockSpec((rows // ni, cols), lambda i, j: (i, 0))


def _ffn(x, norm_g, w_gate_up, w_down, cast_along=(), w_in=None, *, tm=1024, tf=512):
    m = x.shape[0]
    ni, nf = m // tm, D_FF // tf
    cast_specs = [_cast_block_spec(w.shape, ni, nf) for w in cast_along]
    split_in, split_out, split_shapes = [], [], []
    if w_in is not None:
        bands = D_MODEL // BF16_SUBLANES
        assert ni * nf >= bands
        band = lambda i, j: (jnp.minimum(i * nf + j, bands - 1), 0)
        split_in = [pl.BlockSpec((BF16_SUBLANES, w_in.shape[1]), band)]
        widths = (MAIN_WIDTH, SIDE_WIDTH, MEM_WIDTH)
        split_out = [pl.BlockSpec((BF16_SUBLANES, w), band) for w in widths]
        split_shapes = [jax.ShapeDtypeStruct((D_MODEL, w), BF16) for w in widths]
    out = pl.pallas_call(
        functools.partial(_ffn_kernel, len(cast_along), w_in is not None),
        grid=(ni, nf),
        in_specs=[
            pl.BlockSpec((tm, D_MODEL), lambda i, j: (i, 0)),
            _const_spec((1, D_MODEL)),
            pl.BlockSpec((D_MODEL, tf), lambda i, j: (0, j)),
            pl.BlockSpec((D_MODEL, tf), lambda i, j: (0, j + nf)),
            pl.BlockSpec((tf, D_MODEL), lambda i, j: (j, 0)),
        ] + cast_specs + split_in,
        out_specs=[pl.BlockSpec((tm, D_MODEL), lambda i, j: (i, 0))] + cast_specs + split_out,
        out_shape=[jax.ShapeDtypeStruct((m, D_MODEL), F32)]
        + [jax.ShapeDtypeStruct(w.shape, BF16) for w in cast_along] + split_shapes,
        scratch_shapes=[pltpu.VMEM((tm, D_MODEL), BF16)],
        compiler_params=pltpu.CompilerParams(
            dimension_semantics=("parallel", "arbitrary"),
            vmem_limit_bytes=FFN_VMEM_LIMIT),
        name="ffn",
    )(x, norm_g, w_gate_up, w_gate_up, w_down, *cast_along,
      *(() if w_in is None else (w_in,)))
    return out[0] if len(out) == 1 else out


def _inproj_kernel(x_ref, g_ref, w_ref, ws_ref, wq_ref, o_ref, os_ref, oq_ref, h_ref):
    j = pl.program_id(1)

    @pl.when(j == 0)
    def _():
        x = x_ref[...]
        h_ref[...] = (x * _rms_scale(x) * g_ref[...]).astype(BF16)

    h = h_ref[...]
    o_ref[...] = jnp.dot(h, w_ref[...], preferred_element_type=F32).astype(BF16)

    @pl.when(j == pl.num_programs(1) - 1)
    def _():
        os_ref[...] = jnp.dot(h, ws_ref[...], preferred_element_type=F32)
        oq_ref[...] = jnp.dot(h, wq_ref[...], preferred_element_type=F32).astype(BF16)


def _inproj(x, norm_g, w_main, w_side, w_qm, *, tm=1024, tn=1024):
    m = x.shape[0]
    return pl.pallas_call(
        _inproj_kernel,
        grid=(m // tm, MAIN_WIDTH // tn),
        in_specs=[
            pl.BlockSpec((tm, D_MODEL), lambda i, j: (i, 0)),
            _const_spec((1, D_MODEL)),
            pl.BlockSpec((D_MODEL, tn), lambda i, j: (0, j)),
            _const_spec((D_MODEL, SIDE_WIDTH)),
            _const_spec((D_MODEL, MEM_WIDTH)),
        ],
        out_specs=[
            pl.BlockSpec((tm, tn), lambda i, j: (i, j)),
            pl.BlockSpec((tm, SIDE_WIDTH), lambda i, j: (i, 0)),
            pl.BlockSpec((tm, MEM_WIDTH), lambda i, j: (i, 0)),
        ],
        out_shape=[
            jax.ShapeDtypeStruct((m, MAIN_WIDTH), BF16),
            jax.ShapeDtypeStruct((m, SIDE_WIDTH), F32),
            jax.ShapeDtypeStruct((m, MEM_WIDTH), BF16),
        ],
        scratch_shapes=[pltpu.VMEM((tm, D_MODEL), BF16)],
        compiler_params=pltpu.CompilerParams(
            dimension_semantics=("parallel", "arbitrary"),
            vmem_limit_bytes=VMEM_LIMIT),
        name="inproj",
    )(x, norm_g, w_main, w_side, w_qm)


def _memkv_kernel(m_ref, g_ref, w_ref, kg_ref, k_ref, v_ref):
    x = m_ref[...]
    h = (x * _rms_scale(x) * g_ref[...]).astype(BF16)
    kv = jnp.dot(h, w_ref[...], preferred_element_type=F32)
    for hd in range(MEM_HEADS):
        sl = slice(hd * MEM_HEAD_DIM, (hd + 1) * MEM_HEAD_DIM)
        kh = kv[:, sl]
        k_ref[:, sl] = (kh * _rms_scale(kh) * kg_ref[...]).astype(BF16)
    v_ref[...] = kv[:, MEM_WIDTH:].astype(BF16)


def _memkv(mem, norm_g, w_kv, k_gain, *, tm=512):
    m = mem.shape[0]
    return pl.pallas_call(
        _memkv_kernel,
        grid=(m // tm,),
        in_specs=[
            pl.BlockSpec((tm, D_MODEL), lambda i: (i, 0)),
            _const_spec((1, D_MODEL)),
            _const_spec((D_MODEL, 2 * MEM_WIDTH)),
            _const_spec((1, MEM_HEAD_DIM)),
        ],
        out_specs=[
            pl.BlockSpec((tm, MEM_WIDTH), lambda i: (i, 0)),
            pl.BlockSpec((tm, MEM_WIDTH), lambda i: (i, 0)),
        ],
        out_shape=[
            jax.ShapeDtypeStruct((m, MEM_WIDTH), BF16),
            jax.ShapeDtypeStruct((m, MEM_WIDTH), BF16),
        ],
        compiler_params=pltpu.CompilerParams(
            dimension_semantics=("parallel",), vmem_limit_bytes=VMEM_LIMIT),
        name="memkv",
    )(mem, norm_g, w_kv, k_gain)


CUM_CHUNK = 256
N_PARTS = 3
ONES_LANE = N_PARTS * FOX_HEADS


def _split3(x):
    hi = x.astype(BF16)
    r = x - hi.astype(F32)
    mid = r.astype(BF16)
    lo = (r - mid.astype(F32)).astype(BF16)
    return hi, mid, lo


def _forget_cumsum_kernel(f_ref, b_ref, o_ref):
    seq = f_ref.shape[0]
    row = lax.broadcasted_iota(jnp.int32, (CUM_CHUNK, CUM_CHUNK), 0)
    col = lax.broadcasted_iota(jnp.int32, (CUM_CHUNK, CUM_CHUNK), 1)
    tri = jnp.where(row >= col, 1.0, 0.0).astype(BF16)
    lane = lax.broadcasted_iota(jnp.int32, (1, LANES), 1)
    group = jnp.right_shift(lane, FOX_HEADS.bit_length() - 1)
    one = jnp.ones((CUM_CHUNK, LANES), BF16)
    carry = jnp.zeros((1, LANES), F32)
    for ci in range(seq // CUM_CHUNK):
        z = f_ref[ci * CUM_CHUNK:(ci + 1) * CUM_CHUNK, :] + b_ref[...]
        logf = jnp.minimum(z, 0.0) - jnp.log(1.0 + jnp.exp(-jnp.abs(z)))
        c = carry
        for part in _split3(logf):
            c = c + jnp.dot(tri, part, preferred_element_type=F32)
        carry = c[CUM_CHUNK - 1:CUM_CHUNK, :]
        hi, mid, lo = _split3(c)
        o_ref[ci * CUM_CHUNK:(ci + 1) * CUM_CHUNK, :] = jnp.where(
            lane == ONES_LANE, one, jnp.where(group == 0, hi, jnp.where(group == 1, mid, lo)))


def _forget_cumsum(side, b_pad, *, batch, seq):
    side3 = side.reshape(batch, seq, SIDE_WIDTH)
    return pl.pallas_call(
        _forget_cumsum_kernel,
        grid=(batch,),
        in_specs=[
            pl.BlockSpec((None, seq, LANES), lambda b: (b, 0, POOL_WIDTH // LANES)),
            _const_spec((1, LANES)),
        ],
        out_specs=pl.BlockSpec((None, seq, LANES), lambda b: (b, 0, 0)),
        out_shape=jax.ShapeDtypeStruct((batch, seq, LANES), BF16),
        compiler_params=pltpu.CompilerParams(
            dimension_semantics=("parallel",), vmem_limit_bytes=VMEM_LIMIT),
        name="forget_cumsum",
    )(side3, b_pad)


FOX_BLOCK = 256


FOX_PAIRS_PER_STEP = 2
FOX_SCRATCH_PER_PAIR = 7


def _fox_kernel(q_ref, k_ref, v_ref, c_ref, qg_ref, kg_ref, o_ref, *scratch):
    for i in range(FOX_PAIRS_PER_STEP):
        lanes = slice(i * LANES, (i + 1) * LANES)
        _fox_pair(pl.program_id(1) * FOX_PAIRS_PER_STEP + i,
                  q_ref.at[:, lanes], k_ref.at[:, lanes], v_ref.at[:, lanes], c_ref,
                  qg_ref, kg_ref, o_ref.at[:, lanes],
                  *scratch[i * FOX_SCRATCH_PER_PAIR:(i + 1) * FOX_SCRATCH_PER_PAIR])


def _fox_pair(pair, q_ref, k_ref, v_ref, c_ref, qg_ref, kg_ref, o_ref,
              qa_ref, qb_ref, kat_ref, kbt_ref, va_ref, vb_ref, extra_ref):
    seq = q_ref.shape[0]
    blk = FOX_BLOCK
    nblk = seq // blk
    lane = lax.broadcasted_iota(jnp.int32, (1, LANES), 1)
    head_a = lane < FOX_HEAD_DIM
    head_b = jnp.logical_not(head_a)

    r = lax.broadcasted_iota(jnp.int32, (LANES, 4 * LANES), 0)
    c = lax.broadcasted_iota(jnp.int32, (LANES, 4 * LANES), 1)
    operand = jnp.right_shift(c, LANES.bit_length() - 1)
    is_query = operand < 2
    second = jnp.bitwise_and(operand, 1) == 1
    feat_lane = jnp.bitwise_and(c, LANES - 1) - jnp.where(second, 0, FOX_HEAD_DIM)
    var_lane = feat_lane - jnp.where(is_query, 0, N_PARTS)
    one_lane = feat_lane - jnp.where(is_query, N_PARTS, 0)
    part = jnp.right_shift(r, FOX_HEADS.bit_length() - 1)
    part_row = ((part < N_PARTS) & (var_lane == part)
                & (jnp.bitwise_and(r, FOX_HEADS - 1) == 2 * pair + second.astype(jnp.int32)))
    one_row = (r == ONES_LANE) & (one_lane >= 0) & (one_lane < N_PARTS)
    sel = jnp.where(part_row, jnp.where(is_query, 1.0, -1.0),
                    jnp.where(one_row, 1.0, 0.0)).astype(BF16)

    q_gain = qg_ref[...] * FOX_HEAD_DIM ** -0.5
    k_gain = kg_ref[...]
    gains = ((jnp.where(head_a, q_gain, 0.0), jnp.where(head_b, q_gain, 0.0)),
             (jnp.where(head_a, k_gain, 0.0), jnp.where(head_b, k_gain, 0.0)))
    ones_a = jnp.broadcast_to(jnp.where(head_a, 1.0, 0.0).astype(BF16), (blk, LANES))
    ones_b = jnp.broadcast_to(jnp.where(head_b, 1.0, 0.0).astype(BF16), (blk, LANES))

    extra_ref[...] = jnp.dot(c_ref[...], sel, preferred_element_type=F32).astype(BF16)

    def head_rms_scale(x):
        sq = x * x
        s_a = jnp.sum(jnp.where(head_a, sq, 0.0), axis=-1, keepdims=True)
        s_b = jnp.sum(jnp.where(head_a, 0.0, sq), axis=-1, keepdims=True)
        return lax.rsqrt(jnp.where(head_a, s_a, s_b) * (1.0 / FOX_HEAD_DIM) + EPS)

    def prepare(rows):
        q = q_ref[rows, :].astype(F32)
        k = k_ref[rows, :].astype(F32)
        extra = extra_ref[rows, :].astype(F32)
        qn = q * head_rms_scale(q)
        kn = k * head_rms_scale(k)
        for hd, (q_aug, kt_aug) in enumerate(((qa_ref, kat_ref), (qb_ref, kbt_ref))):
            q_aug[rows, :] = (qn * gains[0][hd]
                              + extra[:, hd * LANES:(hd + 1) * LANES]).astype(BF16)
            k_aug = kn * gains[1][hd] + extra[:, (2 + hd) * LANES:(3 + hd) * LANES]
            kt_aug[:, rows] = k_aug.T.astype(BF16)
        v = v_ref[rows, :]
        zero = jnp.zeros_like(v)
        va_ref[rows, 0:LANES] = jnp.where(head_a, v, zero)
        vb_ref[rows, 0:LANES] = jnp.where(head_b, v, zero)
        va_ref[rows, LANES:] = ones_a
        vb_ref[rows, LANES:] = ones_b

    row = lax.broadcasted_iota(jnp.int32, (blk, blk), 0)
    col = lax.broadcasted_iota(jnp.int32, (blk, blk), 1)
    causal = row >= col

    prepare(slice(0, blk))
    for qi in range(nblk):
        q_rows = slice(qi * blk, (qi + 1) * blk)
        past = slice(0, qi * blk)
        if qi + 1 < nblk:
            prepare(slice((qi + 1) * blk, (qi + 2) * blk))
        acc = None
        for q_aug, kt_aug, v_aug in ((qa_ref, kat_ref, va_ref), (qb_ref, kbt_ref, vb_ref)):
            q = q_aug[q_rows, :]
            s_diag = jnp.dot(q, kt_aug[:, q_rows], preferred_element_type=F32)
            s_diag = jnp.where(causal, s_diag, -jnp.inf)
            m = jnp.max(s_diag, axis=-1, keepdims=True)
            if qi > 0:
                s_past = jnp.dot(q, kt_aug[:, past], preferred_element_type=F32)
                m = jnp.maximum(m, jnp.max(s_past, axis=-1, keepdims=True))
                p_past = jnp.exp(s_past - m).astype(BF16)
                part = jnp.dot(p_past, v_aug[past, :], preferred_element_type=F32)
                acc = part if acc is None else acc + part
            p_diag = jnp.exp(s_diag - m).astype(BF16)
            part = jnp.dot(p_diag, v_aug[q_rows, :], preferred_element_type=F32)
            acc = part if acc is None else acc + part
        o_ref[q_rows, :] = (acc[:, :LANES] / acc[:, LANES:]).astype(BF16)


def _fox_attention(main3, c_all, q_gain2, k_gain2):
    batch, seq, _ = main3.shape
    tile = pltpu.VMEM((seq, LANES), BF16)
    tile_t = pltpu.VMEM((LANES, seq), BF16)
    wide = pltpu.VMEM((seq, 2 * LANES), BF16)
    width = FOX_PAIRS_PER_STEP * LANES
    qc, kc, vc = Q_COL // width, K_COL // width, V_COL // width
    per_pair = [tile, tile, tile_t, tile_t, wide, wide, pltpu.VMEM((seq, 4 * LANES), BF16)]
    assert len(per_pair) == FOX_SCRATCH_PER_PAIR
    return pl.pallas_call(
        _fox_kernel,
        grid=(batch, HEAD_PAIRS // FOX_PAIRS_PER_STEP),
        in_specs=[
            pl.BlockSpec((None, seq, width), lambda b, p: (b, 0, qc + p)),
            pl.BlockSpec((None, seq, width), lambda b, p: (b, 0, kc + p)),
            pl.BlockSpec((None, seq, width), lambda b, p: (b, 0, vc + p)),
            pl.BlockSpec((None, seq, LANES), lambda b, p: (b, 0, 0)),
            _const_spec((1, LANES)),
            _const_spec((1, LANES)),
        ],
        out_specs=pl.BlockSpec((None, seq, width), lambda b, p: (b, 0, p)),
        out_shape=jax.ShapeDtypeStruct((batch, seq, FOX_WIDTH), BF16),
        scratch_shapes=per_pair * FOX_PAIRS_PER_STEP,
        compiler_params=pltpu.CompilerParams(
            dimension_semantics=("parallel", "parallel"),
            vmem_limit_bytes=VMEM_LIMIT),
        name="fox_attention",
    )(main3, main3, main3, c_all, q_gain2, k_gain2)


MIX_SLAB = 512


def _mix_kernel(seq, u_ref, halo_ref, gp_ref, gf_ref, gm_ref, qm_ref,
                km_ref, vm_ref, of_ref, pw_ref, ps_ref, wpu_ref, wfo_ref,
                wmo_ref, qg_ref, o_ref, ubuf_ref):
    tm = u_ref.shape[0]
    pos0 = (pl.program_id(0) * tm) % seq

    n_slabs = D_MODEL // MIX_SLAB
    slab = lambda s: slice(s * MIX_SLAB, (s + 1) * MIX_SLAB)

    u = u_ref[...]
    halo_valid = (pos0 > 0).astype(F32)
    ubuf_ref[0:POOL_HALO, :] = halo_ref[...] * halo_valid
    ubuf_ref[POOL_HALO:, :] = u
    pos = pos0 + lax.broadcasted_iota(jnp.int32, (tm, 1), 0)

    def pool_group(g):
        win = POOL_WINDOWS[g]
        sl = slice(g * POOL_GROUP_DIM, (g + 1) * POOL_GROUP_DIM)
        tot = u[:, sl]
        for d in range(1, win):
            tot = tot + ubuf_ref[POOL_HALO - d:POOL_HALO - d + tm, sl]
        count = jnp.minimum(pos + 1, win).astype(F32)
        diff = tot / count - u[:, sl]
        mg = jnp.dot(diff.astype(BF16), pw_ref[g], preferred_element_type=F32)
        return (mg * ps_ref[:, sl]).astype(BF16)

    nt = (((1,), (1,)), ((), ()))

    def mem_head(hd):
        sl = slice(hd * MEM_HEAD_DIM, (hd + 1) * MEM_HEAD_DIM)
        q = qm_ref[:, sl].astype(F32)
        qn = (q * _rms_scale(q) * qg_ref[...]).astype(BF16)
        s = lax.dot_general(qn, km_ref[:, sl], nt, preferred_element_type=F32)
        s = s * (MEM_HEAD_DIM ** -0.5)
        p = jnp.exp(s - jnp.max(s, axis=-1, keepdims=True))
        p = p / jnp.sum(p, axis=-1, keepdims=True)
        return jnp.dot(p.astype(BF16), vm_ref[:, sl],
                       preferred_element_type=F32).astype(BF16)

    mixed = jnp.concatenate([pool_group(g) for g in range(POOL_GROUPS)], axis=-1)
    heads = jnp.concatenate([mem_head(hd) for hd in range(MEM_HEADS)], axis=-1)
    o_fox = of_ref[...]

    def gate(ref, cols):
        return 0.5 * jnp.tanh(0.5 * ref[:, cols].astype(F32)) + 0.5

    for s in range(n_slabs):
        cols = slab(s)
        y_pool = jnp.dot(mixed, wpu_ref[:, cols], preferred_element_type=F32)
        y_fox = jnp.dot(o_fox, wfo_ref[:, cols], preferred_element_type=F32)
        y_mem = jnp.dot(heads, wmo_ref[:, cols], preferred_element_type=F32)
        merged = (gate(gp_ref, cols) * y_pool + gate(gf_ref, cols) * y_fox
                  + gate(gm_ref, cols) * y_mem)
        o_ref[:, cols] = merged.astype(BF16)


def _mix(side, main, q_m, km, vm, o_fox, pool_w, pool_scale, w_pool_up, w_fox_o,
         w_mem_o, q_gain, *, seq, tm=512):
    m = side.shape[0]
    mem_len = km.shape[1]
    halo_blocks = tm // POOL_HALO
    gate_blk = D_MODEL
    return pl.pallas_call(
        functools.partial(_mix_kernel, seq),
        grid=(m // tm,),
        in_specs=[
            pl.BlockSpec((tm, POOL_WIDTH), lambda i: (i, 0)),
            pl.BlockSpec((POOL_HALO, POOL_WIDTH),
                         lambda i: (jnp.maximum(i * halo_blocks - 1, 0), 0)),
            pl.BlockSpec((tm, gate_blk), lambda i: (i, 0)),
            pl.BlockSpec((tm, gate_blk), lambda i: (i, 1)),
            pl.BlockSpec((tm, gate_blk), lambda i: (i, 2)),
            pl.BlockSpec((tm, MEM_WIDTH), lambda i: (i, 0)),
            pl.BlockSpec((None, mem_len, MEM_WIDTH), lambda i: ((i * tm) // seq, 0, 0)),
            pl.BlockSpec((None, mem_len, MEM_WIDTH), lambda i: ((i * tm) // seq, 0, 0)),
            pl.BlockSpec((tm, FOX_WIDTH), lambda i: (i, 0)),
            _const_spec((POOL_GROUPS, POOL_GROUP_DIM, POOL_GROUP_DIM)),
            _const_spec((1, POOL_WIDTH)),
            _const_spec((POOL_WIDTH, D_MODEL)),
            _const_spec((FOX_WIDTH, D_MODEL)),
            _const_spec((MEM_WIDTH, D_MODEL)),
            _const_spec((1, MEM_HEAD_DIM)),
        ],
        out_specs=pl.BlockSpec((tm, D_MODEL), lambda i: (i, 0)),
        out_shape=jax.ShapeDtypeStruct((m, D_MODEL), BF16),
        scratch_shapes=[pltpu.VMEM((tm + POOL_HALO, POOL_WIDTH), F32)],
        compiler_params=pltpu.CompilerParams(
            dimension_semantics=("parallel",), vmem_limit_bytes=VMEM_LIMIT),
        name="mix",
    )(side, side, main, main, main, q_m, km, vm, o_fox, pool_w, pool_scale,
      w_pool_up, w_fox_o, w_mem_o, q_gain)


def _outproj_kernel(x_ref, m_ref, w_ref, o_ref):
    o_ref[...] = x_ref[...] + jnp.dot(m_ref[...], w_ref[...], preferred_element_type=F32)


def _outproj(x, merged, w_out, *, tm=512):
    m = x.shape[0]
    return pl.pallas_call(
        _outproj_kernel,
        grid=(m // tm,),
        in_specs=[
            pl.BlockSpec((tm, D_MODEL), lambda i: (i, 0)),
            pl.BlockSpec((tm, D_MODEL), lambda i: (i, 0)),
            _const_spec((D_MODEL, D_MODEL)),
        ],
        out_specs=pl.BlockSpec((tm, D_MODEL), lambda i: (i, 0)),
        out_shape=jax.ShapeDtypeStruct((m, D_MODEL), F32),
        compiler_params=pltpu.CompilerParams(
            dimension_semantics=("parallel",), vmem_limit_bytes=VMEM_LIMIT),
        name="outproj",
    )(x, merged, w_out)


def kernel(x, mem, ffn1_norm, ffn1_w_gate_up, ffn1_w_down, mix_norm, mem_norm, w_in,
           b_forget, pool_w, pool_scale, w_pool_up, fox_q_norm, fox_k_norm, w_fox_o,
           w_mem_kv, mem_q_norm, mem_k_norm, w_mem_o, w_out,
           ffn2_norm, ffn2_w_gate_up, ffn2_w_down):
    batch, seq, _ = x.shape
    mem_len = mem.shape[1]
    depth = w_in.shape[0]
    xf = x.reshape(batch * seq, D_MODEL)
    memf = mem.reshape(batch * mem_len, D_MODEL)
    row = lambda v: v.reshape(1, -1).astype(F32)

    for l in range(depth):
        b_pad = jnp.pad(jnp.tile(b_forget[l].astype(F32), N_PARTS),
                        (0, LANES - N_PARTS * FOX_HEADS)).reshape(1, LANES)

        xf, w2_gate_up, w2_down, w_main, w_side, w_qm = _ffn(
            xf, row(ffn1_norm[l]), ffn1_w_gate_up[l].astype(BF16),
            ffn1_w_down[l].astype(BF16),
            cast_along=(ffn2_w_gate_up[l], ffn2_w_down[l]), w_in=w_in[l])

        main, side, q_m = _inproj(xf, row(mix_norm[l]), w_main, w_side, w_qm)
        km, vm = _memkv(memf, row(mem_norm[l]), w_mem_kv[l].astype(BF16),
                        row(mem_k_norm[l]))

        c_all = _forget_cumsum(side, b_pad, batch=batch, seq=seq)
        o_fox = _fox_attention(main.reshape(batch, seq, MAIN_WIDTH), c_all,
                               row(jnp.tile(fox_q_norm[l], 2)),
                               row(jnp.tile(fox_k_norm[l], 2)))

        merged = _mix(side, main, q_m, km.reshape(batch, mem_len, MEM_WIDTH),
                      vm.reshape(batch, mem_len, MEM_WIDTH),
                      o_fox.reshape(batch * seq, FOX_WIDTH),
                      pool_w[l].astype(BF16), row(pool_scale[l]),
                      w_pool_up[l].astype(BF16), w_fox_o[l].astype(BF16),
                      w_mem_o[l].astype(BF16), row(mem_q_norm[l]), seq=seq)
        xf = _outproj(xf, merged, w_out[l].astype(BF16))

        xf = _ffn(xf, row(ffn2_norm[l]), w2_gate_up, w2_down)
    return xf.reshape(batch, seq, D_MODEL)
```

```python
import functools

import jax
import jax.numpy as jnp
from jax import lax
from jax.experimental import pallas as pl
from jax.experimental.pallas import tpu as pltpu

F32 = jnp.float32
BF16 = jnp.bfloat16

D_MODEL = 2048
D_FF = 5632
POOL_GROUPS = 4
POOL_GROUP_DIM = 128
POOL_WIDTH = POOL_GROUPS * POOL_GROUP_DIM
POOL_WINDOWS = (2, 4, 8, 16)
POOL_HALO = 16
FOX_HEADS = 16
FOX_HEAD_DIM = 64
FOX_WIDTH = FOX_HEADS * FOX_HEAD_DIM
MEM_HEADS = 4
MEM_HEAD_DIM = 128
MEM_WIDTH = MEM_HEADS * MEM_HEAD_DIM
GATE_WIDTH = 3 * D_MODEL
EPS = 1e-6

LANES = 128
HEAD_PAIRS = FOX_WIDTH // LANES

MAIN_WIDTH = GATE_WIDTH + 3 * FOX_WIDTH
Q_COL, K_COL, V_COL = GATE_WIDTH, GATE_WIDTH + FOX_WIDTH, GATE_WIDTH + 2 * FOX_WIDTH
SIDE_WIDTH = POOL_WIDTH + LANES

VMEM_LIMIT = 56 * 1024 * 1024
FFN_VMEM_LIMIT = 58 * 1024 * 1024


def _rms_scale(x):
    return lax.rsqrt(jnp.mean(x * x, axis=-1, keepdims=True) + EPS)


def _const_spec(shape):
    n = len(shape)
    return pl.BlockSpec(shape, lambda *_: (0,) * n, pipeline_mode=pl.Buffered(1))


BF16_SUBLANES = 16


def _ffn_kernel(n_cast, x_ref, g_ref, wg_ref, wu_ref, wd_ref, *refs):
    cast_in, o_ref, cast_out, h_ref = (refs[:n_cast], refs[n_cast],
                                       refs[n_cast + 1:2 * n_cast + 1], refs[-1])
    j = pl.program_id(1)

    @pl.when(j == 0)
    def _():
        x = x_ref[...]
        h_ref[...] = (x * _rms_scale(x) * g_ref[...]).astype(BF16)
        o_ref[...] = x

    h = h_ref[...]
    gate = jnp.dot(h, wg_ref[...], preferred_element_type=F32)
    for src, dst in zip(cast_in, cast_out):
        dst[...] = src[...].astype(BF16)
    up = jnp.dot(h, wu_ref[...], preferred_element_type=F32)
    act = (0.5 * gate * jax.nn.sigmoid(gate) * up).astype(BF16)
    o_ref[...] += jnp.dot(act, wd_ref[...], preferred_element_type=F32)


def _cast_block_spec(shape, ni, nj):
    rows, cols = shape
    steps = ni * nj
    if rows % (steps * BF16_SUBLANES) == 0:
        return pl.BlockSpec((rows // steps, cols), lambda i, j: (i * nj + j, 0))
    if (rows % (ni * BF16_SUBLANES) == 0 and cols % (nj * LANES) == 0):
        return pl.BlockSpec((rows // ni, cols // nj), lambda i, j: (i, j))
    assert rows % (ni * BF16_SUBLANES) == 0, shape
    return pl.BlockSpec((rows // ni, cols), lambda i, j: (i, 0))


def _ffn(x, norm_g, w_gate_up, w_down, cast_along=(), *, tm=1024, tf=512):
    m = x.shape[0]
    ni, nf = m // tm, D_FF // tf
    cast_specs = [_cast_block_spec(w.shape, ni, nf) for w in cast_along]
    out = pl.pallas_call(
        functools.partial(_ffn_kernel, len(cast_along)),
        grid=(ni, nf),
        in_specs=[
            pl.BlockSpec((tm, D_MODEL), lambda i, j: (i, 0)),
            _const_spec((1, D_MODEL)),
            pl.BlockSpec((D_MODEL, tf), lambda i, j: (0, j)),
            pl.BlockSpec((D_MODEL, tf), lambda i, j: (0, j + nf)),
            pl.BlockSpec((tf, D_MODEL), lambda i, j: (j, 0)),
        ] + cast_specs,
        out_specs=[pl.BlockSpec((tm, D_MODEL), lambda i, j: (i, 0))] + cast_specs,
        out_shape=[jax.ShapeDtypeStruct((m, D_MODEL), F32)]
        + [jax.ShapeDtypeStruct(w.shape, BF16) for w in cast_along],
        scratch_shapes=[pltpu.VMEM((tm, D_MODEL), BF16)],
        compiler_params=pltpu.CompilerParams(
            dimension_semantics=("parallel", "arbitrary"),
            vmem_limit_bytes=FFN_VMEM_LIMIT),
        name="ffn",
    )(x, norm_g, w_gate_up, w_gate_up, w_down, *cast_along)
    return out[0] if len(out) == 1 else out


def _inproj_kernel(x_ref, g_ref, w_ref, ws_ref, wq_ref, o_ref, os_ref, oq_ref, h_ref):
    j = pl.program_id(1)
    nt = (((1,), (1,)), ((), ()))

    @pl.when(j == 0)
    def _():
        x = x_ref[...]
        h_ref[...] = (x * _rms_scale(x) * g_ref[...]).astype(BF16)

    h = h_ref[...]
    o_ref[...] = lax.dot_general(h, w_ref[...], nt,
                                 preferred_element_type=F32).astype(BF16)

    @pl.when(j == pl.num_programs(1) - 1)
    def _():
        os_ref[...] = lax.dot_general(h, ws_ref[...], nt, preferred_element_type=F32)
        oq_ref[...] = lax.dot_general(h, wq_ref[...], nt,
                                      preferred_element_type=F32).astype(BF16)


def _inproj(x, norm_g, w_main_t, w_side_t, w_qm_t, *, tm=1024, tn=1024):
    m = x.shape[0]
    return pl.pallas_call(
        _inproj_kernel,
        grid=(m // tm, MAIN_WIDTH // tn),
        in_specs=[
            pl.BlockSpec((tm, D_MODEL), lambda i, j: (i, 0)),
            _const_spec((1, D_MODEL)),
            pl.BlockSpec((tn, D_MODEL), lambda i, j: (j, 0)),
            _const_spec((SIDE_WIDTH, D_MODEL)),
            _const_spec((MEM_WIDTH, D_MODEL)),
        ],
        out_specs=[
            pl.BlockSpec((tm, tn), lambda i, j: (i, j)),
            pl.BlockSpec((tm, SIDE_WIDTH), lambda i, j: (i, 0)),
            pl.BlockSpec((tm, MEM_WIDTH), lambda i, j: (i, 0)),
        ],
        out_shape=[
            jax.ShapeDtypeStruct((m, MAIN_WIDTH), BF16),
            jax.ShapeDtypeStruct((m, SIDE_WIDTH), F32),
            jax.ShapeDtypeStruct((m, MEM_WIDTH), BF16),
        ],
        scratch_shapes=[pltpu.VMEM((tm, D_MODEL), BF16)],
        compiler_params=pltpu.CompilerParams(
            dimension_semantics=("parallel", "arbitrary"),
            vmem_limit_bytes=VMEM_LIMIT),
        name="inproj",
    )(x, norm_g, w_main_t, w_side_t, w_qm_t)


def _memkv_kernel(m_ref, g_ref, w_ref, kg_ref, k_ref, v_ref):
    x = m_ref[...]
    h = (x * _rms_scale(x) * g_ref[...]).astype(BF16)
    kv = jnp.dot(h, w_ref[...], preferred_element_type=F32)
    for hd in range(MEM_HEADS):
        sl = slice(hd * MEM_HEAD_DIM, (hd + 1) * MEM_HEAD_DIM)
        kh = kv[:, sl]
        k_ref[:, sl] = (kh * _rms_scale(kh) * kg_ref[...]).astype(BF16)
    v_ref[...] = kv[:, MEM_WIDTH:].astype(BF16)


def _memkv(mem, norm_g, w_kv, k_gain, *, tm=512):
    m = mem.shape[0]
    return pl.pallas_call(
        _memkv_kernel,
        grid=(m // tm,),
        in_specs=[
            pl.BlockSpec((tm, D_MODEL), lambda i: (i, 0)),
            _const_spec((1, D_MODEL)),
            _const_spec((D_MODEL, 2 * MEM_WIDTH)),
            _const_spec((1, MEM_HEAD_DIM)),
        ],
        out_specs=[
            pl.BlockSpec((tm, MEM_WIDTH), lambda i: (i, 0)),
            pl.BlockSpec((tm, MEM_WIDTH), lambda i: (i, 0)),
        ],
        out_shape=[
            jax.ShapeDtypeStruct((m, MEM_WIDTH), BF16),
            jax.ShapeDtypeStruct((m, MEM_WIDTH), BF16),
        ],
        compiler_params=pltpu.CompilerParams(
            dimension_semantics=("parallel",), vmem_limit_bytes=VMEM_LIMIT),
        name="memkv",
    )(mem, norm_g, w_kv, k_gain)


CUM_CHUNK = 256
N_PARTS = 3
ONES_LANE = N_PARTS * FOX_HEADS


def _split3(x):
    hi = x.astype(BF16)
    r = x - hi.astype(F32)
    mid = r.astype(BF16)
    lo = (r - mid.astype(F32)).astype(BF16)
    return hi, mid, lo


def _forget_cumsum_kernel(f_ref, b_ref, o_ref):
    seq = f_ref.shape[0]
    row = lax.broadcasted_iota(jnp.int32, (CUM_CHUNK, CUM_CHUNK), 0)
    col = lax.broadcasted_iota(jnp.int32, (CUM_CHUNK, CUM_CHUNK), 1)
    tri = jnp.where(row >= col, 1.0, 0.0).astype(BF16)
    lane = lax.broadcasted_iota(jnp.int32, (1, LANES), 1)
    group = jnp.right_shift(lane, FOX_HEADS.bit_length() - 1)
    one = jnp.ones((CUM_CHUNK, LANES), BF16)
    carry = jnp.zeros((1, LANES), F32)
    for ci in range(seq // CUM_CHUNK):
        z = f_ref[ci * CUM_CHUNK:(ci + 1) * CUM_CHUNK, :] + b_ref[...]
        logf = jnp.minimum(z, 0.0) - jnp.log(1.0 + jnp.exp(-jnp.abs(z)))
        c = carry
        for part in _split3(logf):
            c = c + jnp.dot(tri, part, preferred_element_type=F32)
        carry = c[CUM_CHUNK - 1:CUM_CHUNK, :]
        hi, mid, lo = _split3(c)
        o_ref[ci * CUM_CHUNK:(ci + 1) * CUM_CHUNK, :] = jnp.where(
            lane == ONES_LANE, one, jnp.where(group == 0, hi, jnp.where(group == 1, mid, lo)))


def _forget_cumsum(side, b_pad, *, batch, seq):
    side3 = side.reshape(batch, seq, SIDE_WIDTH)
    return pl.pallas_call(
        _forget_cumsum_kernel,
        grid=(batch,),
        in_specs=[
            pl.BlockSpec((None, seq, LANES), lambda b: (b, 0, POOL_WIDTH // LANES)),
            _const_spec((1, LANES)),
        ],
        out_specs=pl.BlockSpec((None, seq, LANES), lambda b: (b, 0, 0)),
        out_shape=jax.ShapeDtypeStruct((batch, seq, LANES), BF16),
        compiler_params=pltpu.CompilerParams(
            dimension_semantics=("parallel",), vmem_limit_bytes=VMEM_LIMIT),
        name="forget_cumsum",
    )(side3, b_pad)


FOX_BLOCK = 256


FOX_PAIRS_PER_STEP = 2
FOX_SCRATCH_PER_PAIR = 7


def _fox_kernel(q_ref, k_ref, v_ref, c_ref, qg_ref, kg_ref, o_ref, *scratch):
    for i in range(FOX_PAIRS_PER_STEP):
        lanes = slice(i * LANES, (i + 1) * LANES)
        _fox_pair(pl.program_id(1) * FOX_PAIRS_PER_STEP + i,
                  q_ref.at[:, lanes], k_ref.at[:, lanes], v_ref.at[:, lanes], c_ref,
                  qg_ref, kg_ref, o_ref.at[:, lanes],
                  *scratch[i * FOX_SCRATCH_PER_PAIR:(i + 1) * FOX_SCRATCH_PER_PAIR])


def _fox_pair(pair, q_ref, k_ref, v_ref, c_ref, qg_ref, kg_ref, o_ref,
              qa_ref, qb_ref, kat_ref, kbt_ref, va_ref, vb_ref, extra_ref):
    seq = q_ref.shape[0]
    blk = FOX_BLOCK
    nblk = seq // blk
    lane = lax.broadcasted_iota(jnp.int32, (1, LANES), 1)
    head_a = lane < FOX_HEAD_DIM
    head_b = jnp.logical_not(head_a)

    r = lax.broadcasted_iota(jnp.int32, (LANES, 4 * LANES), 0)
    c = lax.broadcasted_iota(jnp.int32, (LANES, 4 * LANES), 1)
    operand = jnp.right_shift(c, LANES.bit_length() - 1)
    is_query = operand < 2
    second = jnp.bitwise_and(operand, 1) == 1
    feat_lane = jnp.bitwise_and(c, LANES - 1) - jnp.where(second, 0, FOX_HEAD_DIM)
    var_lane = feat_lane - jnp.where(is_query, 0, N_PARTS)
    one_lane = feat_lane - jnp.where(is_query, N_PARTS, 0)
    part = jnp.right_shift(r, FOX_HEADS.bit_length() - 1)
    part_row = ((part < N_PARTS) & (var_lane == part)
                & (jnp.bitwise_and(r, FOX_HEADS - 1) == 2 * pair + second.astype(jnp.int32)))
    one_row = (r == ONES_LANE) & (one_lane >= 0) & (one_lane < N_PARTS)
    sel = jnp.where(part_row, jnp.where(is_query, 1.0, -1.0),
                    jnp.where(one_row, 1.0, 0.0)).astype(BF16)

    q_gain = qg_ref[...] * FOX_HEAD_DIM ** -0.5
    k_gain = kg_ref[...]
    gains = ((jnp.where(head_a, q_gain, 0.0), jnp.where(head_b, q_gain, 0.0)),
             (jnp.where(head_a, k_gain, 0.0), jnp.where(head_b, k_gain, 0.0)))
    ones_a = jnp.broadcast_to(jnp.where(head_a, 1.0, 0.0).astype(BF16), (blk, LANES))
    ones_b = jnp.broadcast_to(jnp.where(head_b, 1.0, 0.0).astype(BF16), (blk, LANES))

    extra_ref[...] = jnp.dot(c_ref[...], sel, preferred_element_type=F32).astype(BF16)

    def head_rms_scale(x):
        sq = x * x
        s_a = jnp.sum(jnp.where(head_a, sq, 0.0), axis=-1, keepdims=True)
        s_b = jnp.sum(jnp.where(head_a, 0.0, sq), axis=-1, keepdims=True)
        return lax.rsqrt(jnp.where(head_a, s_a, s_b) * (1.0 / FOX_HEAD_DIM) + EPS)

    def prepare(rows):
        q = q_ref[rows, :].astype(F32)
        k = k_ref[rows, :].astype(F32)
        extra = extra_ref[rows, :].astype(F32)
        qn = q * head_rms_scale(q)
        kn = k * head_rms_scale(k)
        for hd, (q_aug, kt_aug) in enumerate(((qa_ref, kat_ref), (qb_ref, kbt_ref))):
            q_aug[rows, :] = (qn * gains[0][hd]
                              + extra[:, hd * LANES:(hd + 1) * LANES]).astype(BF16)
            k_aug = kn * gains[1][hd] + extra[:, (2 + hd) * LANES:(3 + hd) * LANES]
            kt_aug[:, rows] = k_aug.T.astype(BF16)
        v = v_ref[rows, :]
        zero = jnp.zeros_like(v)
        va_ref[rows, 0:LANES] = jnp.where(head_a, v, zero)
        vb_ref[rows, 0:LANES] = jnp.where(head_b, v, zero)
        va_ref[rows, LANES:] = ones_a
        vb_ref[rows, LANES:] = ones_b

    row = lax.broadcasted_iota(jnp.int32, (blk, blk), 0)
    col = lax.broadcasted_iota(jnp.int32, (blk, blk), 1)
    causal = row >= col

    prepare(slice(0, blk))
    for qi in range(nblk):
        q_rows = slice(qi * blk, (qi + 1) * blk)
        past = slice(0, qi * blk)
        if qi + 1 < nblk:
            prepare(slice((qi + 1) * blk, (qi + 2) * blk))
        acc = None
        for q_aug, kt_aug, v_aug in ((qa_ref, kat_ref, va_ref), (qb_ref, kbt_ref, vb_ref)):
            q = q_aug[q_rows, :]
            s_diag = jnp.dot(q, kt_aug[:, q_rows], preferred_element_type=F32)
            s_diag = jnp.where(causal, s_diag, -jnp.inf)
            m = jnp.max(s_diag, axis=-1, keepdims=True)
            if qi > 0:
                s_past = jnp.dot(q, kt_aug[:, past], preferred_element_type=F32)
                m = jnp.maximum(m, jnp.max(s_past, axis=-1, keepdims=True))
                p_past = jnp.exp(s_past - m).astype(BF16)
                part = jnp.dot(p_past, v_aug[past, :], preferred_element_type=F32)
                acc = part if acc is None else acc + part
            p_diag = jnp.exp(s_diag - m).astype(BF16)
            part = jnp.dot(p_diag, v_aug[q_rows, :], preferred_element_type=F32)
            acc = part if acc is None else acc + part
        o_ref[q_rows, :] = (acc[:, :LANES] / acc[:, LANES:]).astype(BF16)


def _fox_attention(main3, c_all, q_gain2, k_gain2):
    batch, seq, _ = main3.shape
    tile = pltpu.VMEM((seq, LANES), BF16)
    tile_t = pltpu.VMEM((LANES, seq), BF16)
    wide = pltpu.VMEM((seq, 2 * LANES), BF16)
    width = FOX_PAIRS_PER_STEP * LANES
    qc, kc, vc = Q_COL // width, K_COL // width, V_COL // width
    per_pair = [tile, tile, tile_t, tile_t, wide, wide, pltpu.VMEM((seq, 4 * LANES), BF16)]
    assert len(per_pair) == FOX_SCRATCH_PER_PAIR
    return pl.pallas_call(
        _fox_kernel,
        grid=(batch, HEAD_PAIRS // FOX_PAIRS_PER_STEP),
        in_specs=[
            pl.BlockSpec((None, seq, width), lambda b, p: (b, 0, qc + p)),
            pl.BlockSpec((None, seq, width), lambda b, p: (b, 0, kc + p)),
            pl.BlockSpec((None, seq, width), lambda b, p: (b, 0, vc + p)),
            pl.BlockSpec((None, seq, LANES), lambda b, p: (b, 0, 0)),
            _const_spec((1, LANES)),
            _const_spec((1, LANES)),
        ],
        out_specs=pl.BlockSpec((None, seq, width), lambda b, p: (b, 0, p)),
        out_shape=jax.ShapeDtypeStruct((batch, seq, FOX_WIDTH), BF16),
        scratch_shapes=per_pair * FOX_PAIRS_PER_STEP,
        compiler_params=pltpu.CompilerParams(
            dimension_semantics=("parallel", "parallel"),
            vmem_limit_bytes=VMEM_LIMIT),
        name="fox_attention",
    )(main3, main3, main3, c_all, q_gain2, k_gain2)


MIX_SLAB = 512


def _mix_kernel(seq, u_ref, halo_ref, gp_ref, gf_ref, gm_ref, qm_ref,
                km_ref, vm_ref, of_ref, pw_ref, ps_ref, wpu_ref, wfo_ref,
                wmo_ref, qg_ref, o_ref, ubuf_ref):
    tm = u_ref.shape[0]
    pos0 = (pl.program_id(0) * tm) % seq

    n_slabs = D_MODEL // MIX_SLAB
    slab = lambda s: slice(s * MIX_SLAB, (s + 1) * MIX_SLAB)

    u = u_ref[...]
    halo_valid = (pos0 > 0).astype(F32)
    ubuf_ref[0:POOL_HALO, :] = halo_ref[...] * halo_valid
    ubuf_ref[POOL_HALO:, :] = u
    pos = pos0 + lax.broadcasted_iota(jnp.int32, (tm, 1), 0)

    def pool_group(g):
        win = POOL_WINDOWS[g]
        sl = slice(g * POOL_GROUP_DIM, (g + 1) * POOL_GROUP_DIM)
        tot = u[:, sl]
        for d in range(1, win):
            tot = tot + ubuf_ref[POOL_HALO - d:POOL_HALO - d + tm, sl]
        count = jnp.minimum(pos + 1, win).astype(F32)
        diff = tot / count - u[:, sl]
        mg = jnp.dot(diff.astype(BF16), pw_ref[g], preferred_element_type=F32)
        return (mg * ps_ref[:, sl]).astype(BF16)

    nt = (((1,), (1,)), ((), ()))

    def mem_head(hd):
        sl = slice(hd * MEM_HEAD_DIM, (hd + 1) * MEM_HEAD_DIM)
        q = qm_ref[:, sl].astype(F32)
        qn = (q * _rms_scale(q) * qg_ref[...]).astype(BF16)
        s = lax.dot_general(qn, km_ref[:, sl], nt, preferred_element_type=F32)
        s = s * (MEM_HEAD_DIM ** -0.5)
        p = jnp.exp(s - jnp.max(s, axis=-1, keepdims=True))
        p = p / jnp.sum(p, axis=-1, keepdims=True)
        return jnp.dot(p.astype(BF16), vm_ref[:, sl],
                       preferred_element_type=F32).astype(BF16)

    mixed = jnp.concatenate([pool_group(g) for g in range(POOL_GROUPS)], axis=-1)
    heads = jnp.concatenate([mem_head(hd) for hd in range(MEM_HEADS)], axis=-1)
    o_fox = of_ref[...]

    def gate(ref, cols):
        return 0.5 * jnp.tanh(0.5 * ref[:, cols].astype(F32)) + 0.5

    for s in range(n_slabs):
        cols = slab(s)
        y_pool = jnp.dot(mixed, wpu_ref[:, cols], preferred_element_type=F32)
        y_fox = jnp.dot(o_fox, wfo_ref[:, cols], preferred_element_type=F32)
        y_mem = jnp.dot(heads, wmo_ref[:, cols], preferred_element_type=F32)
        merged = (gate(gp_ref, cols) * y_pool + gate(gf_ref, cols) * y_fox
                  + gate(gm_ref, cols) * y_mem)
        o_ref[:, cols] = merged.astype(BF16)


def _mix(side, main, q_m, km, vm, o_fox, pool_w, pool_scale, w_pool_up, w_fox_o,
         w_mem_o, q_gain, *, seq, tm=512):
    m = side.shape[0]
    mem_len = km.shape[1]
    halo_blocks = tm // POOL_HALO
    gate_blk = D_MODEL
    return pl.pallas_call(
        functools.partial(_mix_kernel, seq),
        grid=(m // tm,),
        in_specs=[
            pl.BlockSpec((tm, POOL_WIDTH), lambda i: (i, 0)),
            pl.BlockSpec((POOL_HALO, POOL_WIDTH),
                         lambda i: (jnp.maximum(i * halo_blocks - 1, 0), 0)),
            pl.BlockSpec((tm, gate_blk), lambda i: (i, 0)),
            pl.BlockSpec((tm, gate_blk), lambda i: (i, 1)),
            pl.BlockSpec((tm, gate_blk), lambda i: (i, 2)),
            pl.BlockSpec((tm, MEM_WIDTH), lambda i: (i, 0)),
            pl.BlockSpec((None, mem_len, MEM_WIDTH), lambda i: ((i * tm) // seq, 0, 0)),
            pl.BlockSpec((None, mem_len, MEM_WIDTH), lambda i: ((i * tm) // seq, 0, 0)),
            pl.BlockSpec((tm, FOX_WIDTH), lambda i: (i, 0)),
            _const_spec((POOL_GROUPS, POOL_GROUP_DIM, POOL_GROUP_DIM)),
            _const_spec((1, POOL_WIDTH)),
            _const_spec((POOL_WIDTH, D_MODEL)),
            _const_spec((FOX_WIDTH, D_MODEL)),
            _const_spec((MEM_WIDTH, D_MODEL)),
            _const_spec((1, MEM_HEAD_DIM)),
        ],
        out_specs=pl.BlockSpec((tm, D_MODEL), lambda i: (i, 0)),
        out_shape=jax.ShapeDtypeStruct((m, D_MODEL), BF16),
        scratch_shapes=[pltpu.VMEM((tm + POOL_HALO, POOL_WIDTH), F32)],
        compiler_params=pltpu.CompilerParams(
            dimension_semantics=("parallel",), vmem_limit_bytes=VMEM_LIMIT),
        name="mix",
    )(side, side, main, main, main, q_m, km, vm, o_fox, pool_w, pool_scale,
      w_pool_up, w_fox_o, w_mem_o, q_gain)


def _outproj_kernel(x_ref, m_ref, w_ref, o_ref):
    o_ref[...] = x_ref[...] + jnp.dot(m_ref[...], w_ref[...], preferred_element_type=F32)


def _outproj(x, merged, w_out, *, tm=512):
    m = x.shape[0]
    return pl.pallas_call(
        _outproj_kernel,
        grid=(m // tm,),
        in_specs=[
            pl.BlockSpec((tm, D_MODEL), lambda i: (i, 0)),
            pl.BlockSpec((tm, D_MODEL), lambda i: (i, 0)),
            _const_spec((D_MODEL, D_MODEL)),
        ],
        out_specs=pl.BlockSpec((tm, D_MODEL), lambda i: (i, 0)),
        out_shape=jax.ShapeDtypeStruct((m, D_MODEL), F32),
        compiler_params=pltpu.CompilerParams(
            dimension_semantics=("parallel",), vmem_limit_bytes=VMEM_LIMIT),
        name="outproj",
    )(x, merged, w_out)


def kernel(x, mem, ffn1_norm, ffn1_w_gate_up, ffn1_w_down, mix_norm, mem_norm, w_in,
           b_forget, pool_w, pool_scale, w_pool_up, fox_q_norm, fox_k_norm, w_fox_o,
           w_mem_kv, mem_q_norm, mem_k_norm, w_mem_o, w_out,
           ffn2_norm, ffn2_w_gate_up, ffn2_w_down):
    batch, seq, _ = x.shape
    mem_len = mem.shape[1]
    depth = w_in.shape[0]
    xf = x.reshape(batch * seq, D_MODEL)
    memf = mem.reshape(batch * mem_len, D_MODEL)
    row = lambda v: v.reshape(1, -1).astype(F32)

    c_q = POOL_WIDTH
    c_f = POOL_WIDTH + 3 * FOX_WIDTH
    c_qm = c_f + FOX_HEADS
    c_gate = c_qm + MEM_WIDTH

    for l in range(depth):
        wt = jnp.swapaxes(w_in[l], 0, 1)
        w_main_t = jnp.concatenate([wt[c_gate:], wt[c_q:c_f]], axis=0).astype(BF16)
        w_qm_t = wt[c_qm:c_gate].astype(BF16)
        w_side_t = jnp.pad(jnp.concatenate([wt[:POOL_WIDTH]] + [wt[c_f:c_qm]] * N_PARTS,
                                           axis=0),
                           ((0, LANES - N_PARTS * FOX_HEADS), (0, 0))).astype(BF16)
        b_pad = jnp.pad(jnp.tile(b_forget[l].astype(F32), N_PARTS),
                        (0, LANES - N_PARTS * FOX_HEADS)).reshape(1, LANES)

        xf, w2_gate_up, w2_down, w_out_b, w_fox_o_b = _ffn(
            xf, row(ffn1_norm[l]), ffn1_w_gate_up[l].astype(BF16),
            ffn1_w_down[l].astype(BF16),
            cast_along=(ffn2_w_gate_up[l], ffn2_w_down[l], w_out[l], w_fox_o[l]))

        main, side, q_m = _inproj(xf, row(mix_norm[l]), w_main_t, w_side_t, w_qm_t)
        km, vm = _memkv(memf, row(mem_norm[l]), w_mem_kv[l].astype(BF16),
                        row(mem_k_norm[l]))

        c_all = _forget_cumsum(side, b_pad, batch=batch, seq=seq)
        o_fox = _fox_attention(main.reshape(batch, seq, MAIN_WIDTH), c_all,
                               row(jnp.tile(fox_q_norm[l], 2)),
                               row(jnp.tile(fox_k_norm[l], 2)))

        merged = _mix(side, main, q_m, km.reshape(batch, mem_len, MEM_WIDTH),
                      vm.reshape(batch, mem_len, MEM_WIDTH),
                      o_fox.reshape(batch * seq, FOX_WIDTH),
                      pool_w[l].astype(BF16), row(pool_scale[l]),
                      w_pool_up[l].astype(BF16), w_fox_o_b,
                      w_mem_o[l].astype(BF16), row(mem_q_norm[l]), seq=seq)
        xf = _outproj(xf, merged, w_out_b)

        xf = _ffn(xf, row(ffn2_norm[l]), w2_gate_up, w2_down)
    return xf.reshape(batch, seq, D_MODEL)
```

```python
import functools

import jax
import jax.numpy as jnp
from jax import lax
from jax.experimental import pallas as pl
from jax.experimental.pallas import tpu as pltpu

F32 = jnp.float32
BF16 = jnp.bfloat16

D_MODEL = 2048
D_FF = 5632
POOL_GROUPS = 4
POOL_GROUP_DIM = 128
POOL_WIDTH = POOL_GROUPS * POOL_GROUP_DIM
POOL_WINDOWS = (2, 4, 8, 16)
POOL_HALO = 16
FOX_HEADS = 16
FOX_HEAD_DIM = 64
FOX_WIDTH = FOX_HEADS * FOX_HEAD_DIM
MEM_HEADS = 4
MEM_HEAD_DIM = 128
MEM_WIDTH = MEM_HEADS * MEM_HEAD_DIM
GATE_WIDTH = 3 * D_MODEL
EPS = 1e-6

LANES = 128
HEAD_PAIRS = FOX_WIDTH // LANES

MAIN_WIDTH = GATE_WIDTH + 3 * FOX_WIDTH
Q_COL, K_COL, V_COL = GATE_WIDTH, GATE_WIDTH + FOX_WIDTH, GATE_WIDTH + 2 * FOX_WIDTH
SIDE_WIDTH = POOL_WIDTH + LANES

VMEM_LIMIT = 56 * 1024 * 1024
FFN_VMEM_LIMIT = 58 * 1024 * 1024


def _rms_scale(x):
    return lax.rsqrt(jnp.mean(x * x, axis=-1, keepdims=True) + EPS)


def _const_spec(shape):
    n = len(shape)
    return pl.BlockSpec(shape, lambda *_: (0,) * n, pipeline_mode=pl.Buffered(1))


BF16_SUBLANES = 16


def _ffn_kernel(n_cast, x_ref, g_ref, wg_ref, wu_ref, wd_ref, *refs):
    cast_in, o_ref, cast_out, h_ref = (refs[:n_cast], refs[n_cast],
                                       refs[n_cast + 1:2 * n_cast + 1], refs[-1])
    j = pl.program_id(1)

    @pl.when(j == 0)
    def _():
        x = x_ref[...]
        h_ref[...] = (x * _rms_scale(x) * g_ref[...]).astype(BF16)
        o_ref[...] = x

    h = h_ref[...]
    gate = jnp.dot(h, wg_ref[...], preferred_element_type=F32)
    for src, dst in zip(cast_in, cast_out):
        dst[...] = src[...].astype(BF16)
    up = jnp.dot(h, wu_ref[...], preferred_element_type=F32)
    act = (0.5 * gate * jax.nn.sigmoid(gate) * up).astype(BF16)
    o_ref[...] += jnp.dot(act, wd_ref[...], preferred_element_type=F32)


def _cast_block_spec(shape, ni, nj):
    rows, cols = shape
    steps = ni * nj
    if rows % (steps * BF16_SUBLANES) == 0:
        return pl.BlockSpec((rows // steps, cols), lambda i, j: (i * nj + j, 0))
    if (rows % (ni * BF16_SUBLANES) == 0 and cols % (nj * LANES) == 0):
        return pl.BlockSpec((rows // ni, cols // nj), lambda i, j: (i, j))
    assert rows % (ni * BF16_SUBLANES) == 0, shape
    return pl.BlockSpec((rows // ni, cols), lambda i, j: (i, 0))


def _ffn(x, norm_g, w_gate_up, w_down, cast_along=(), *, tm=1024, tf=512):
    m = x.shape[0]
    ni, nf = m // tm, D_FF // tf
    cast_specs = [_cast_block_spec(w.shape, ni, nf) for w in cast_along]
    out = pl.pallas_call(
        functools.partial(_ffn_kernel, len(cast_along)),
        grid=(ni, nf),
        in_specs=[
            pl.BlockSpec((tm, D_MODEL), lambda i, j: (i, 0)),
            _const_spec((1, D_MODEL)),
            pl.BlockSpec((D_MODEL, tf), lambda i, j: (0, j)),
            pl.BlockSpec((D_MODEL, tf), lambda i, j: (0, j + nf)),
            pl.BlockSpec((tf, D_MODEL), lambda i, j: (j, 0)),
        ] + cast_specs,
        out_specs=[pl.BlockSpec((tm, D_MODEL), lambda i, j: (i, 0))] + cast_specs,
        out_shape=[jax.ShapeDtypeStruct((m, D_MODEL), F32)]
        + [jax.ShapeDtypeStruct(w.shape, BF16) for w in cast_along],
        scratch_shapes=[pltpu.VMEM((tm, D_MODEL), BF16)],
        compiler_params=pltpu.CompilerParams(
            dimension_semantics=("parallel", "arbitrary"),
            vmem_limit_bytes=FFN_VMEM_LIMIT),
        name="ffn",
    )(x, norm_g, w_gate_up, w_gate_up, w_down, *cast_along)
    return out[0] if len(out) == 1 else out


def _inproj_kernel(x_ref, g_ref, w_ref, ws_ref, wq_ref, o_ref, os_ref, oq_ref, h_ref):
    j = pl.program_id(1)
    nt = (((1,), (1,)), ((), ()))

    @pl.when(j == 0)
    def _():
        x = x_ref[...]
        h_ref[...] = (x * _rms_scale(x) * g_ref[...]).astype(BF16)

    h = h_ref[...]
    o_ref[...] = lax.dot_general(h, w_ref[...], nt,
                                 preferred_element_type=F32).astype(BF16)

    @pl.when(j == pl.num_programs(1) - 1)
    def _():
        os_ref[...] = lax.dot_general(h, ws_ref[...], nt, preferred_element_type=F32)
        oq_ref[...] = lax.dot_general(h, wq_ref[...], nt,
                                      preferred_element_type=F32).astype(BF16)


def _inproj(x, norm_g, w_in_t, w_side_t, *, tm=1024, tn=1024):
    m = x.shape[0]
    c_qkv = POOL_WIDTH
    c_qm = POOL_WIDTH + 3 * FOX_WIDTH + FOX_HEADS
    c_gate = c_qm + MEM_WIDTH
    gate_tiles = GATE_WIDTH // tn
    assert c_qkv % BF16_SUBLANES == c_qm % BF16_SUBLANES == c_gate % BF16_SUBLANES == 0

    def main_rows(i, j):
        start = jnp.where(j < gate_tiles, c_gate + j * tn, c_qkv + (j - gate_tiles) * tn)
        return (pl.multiple_of(start, BF16_SUBLANES), 0)

    return pl.pallas_call(
        _inproj_kernel,
        grid=(m // tm, MAIN_WIDTH // tn),
        in_specs=[
            pl.BlockSpec((tm, D_MODEL), lambda i, j: (i, 0)),
            _const_spec((1, D_MODEL)),
            pl.BlockSpec((pl.Element(tn), pl.Element(D_MODEL)), main_rows),
            _const_spec((SIDE_WIDTH, D_MODEL)),
            pl.BlockSpec((pl.Element(MEM_WIDTH), pl.Element(D_MODEL)),
                         lambda i, j: (c_qm, 0), pipeline_mode=pl.Buffered(1)),
        ],
        out_specs=[
            pl.BlockSpec((tm, tn), lambda i, j: (i, j)),
            pl.BlockSpec((tm, SIDE_WIDTH), lambda i, j: (i, 0)),
            pl.BlockSpec((tm, MEM_WIDTH), lambda i, j: (i, 0)),
        ],
        out_shape=[
            jax.ShapeDtypeStruct((m, MAIN_WIDTH), BF16),
            jax.ShapeDtypeStruct((m, SIDE_WIDTH), F32),
            jax.ShapeDtypeStruct((m, MEM_WIDTH), BF16),
        ],
        scratch_shapes=[pltpu.VMEM((tm, D_MODEL), BF16)],
        compiler_params=pltpu.CompilerParams(
            dimension_semantics=("parallel", "arbitrary"),
            vmem_limit_bytes=VMEM_LIMIT),
        name="inproj",
    )(x, norm_g, w_in_t, w_side_t, w_in_t)


def _memkv_kernel(m_ref, g_ref, w_ref, kg_ref, k_ref, v_ref):
    x = m_ref[...]
    h = (x * _rms_scale(x) * g_ref[...]).astype(BF16)
    kv = jnp.dot(h, w_ref[...], preferred_element_type=F32)
    for hd in range(MEM_HEADS):
        sl = slice(hd * MEM_HEAD_DIM, (hd + 1) * MEM_HEAD_DIM)
        kh = kv[:, sl]
        k_ref[:, sl] = (kh * _rms_scale(kh) * kg_ref[...]).astype(BF16)
    v_ref[...] = kv[:, MEM_WIDTH:].astype(BF16)


def _memkv(mem, norm_g, w_kv, k_gain, *, tm=512):
    m = mem.shape[0]
    return pl.pallas_call(
        _memkv_kernel,
        grid=(m // tm,),
        in_specs=[
            pl.BlockSpec((tm, D_MODEL), lambda i: (i, 0)),
            _const_spec((1, D_MODEL)),
            _const_spec((D_MODEL, 2 * MEM_WIDTH)),
            _const_spec((1, MEM_HEAD_DIM)),
        ],
        out_specs=[
            pl.BlockSpec((tm, MEM_WIDTH), lambda i: (i, 0)),
            pl.BlockSpec((tm, MEM_WIDTH), lambda i: (i, 0)),
        ],
        out_shape=[
            jax.ShapeDtypeStruct((m, MEM_WIDTH), BF16),
            jax.ShapeDtypeStruct((m, MEM_WIDTH), BF16),
        ],
        compiler_params=pltpu.CompilerParams(
            dimension_semantics=("parallel",), vmem_limit_bytes=VMEM_LIMIT),
        name="memkv",
    )(mem, norm_g, w_kv, k_gain)


CUM_CHUNK = 256
N_PARTS = 3
ONES_LANE = N_PARTS * FOX_HEADS


def _split3(x):
    hi = x.astype(BF16)
    r = x - hi.astype(F32)
    mid = r.astype(BF16)
    lo = (r - mid.astype(F32)).astype(BF16)
    return hi, mid, lo


def _forget_cumsum_kernel(f_ref, b_ref, o_ref):
    seq = f_ref.shape[0]
    row = lax.broadcasted_iota(jnp.int32, (CUM_CHUNK, CUM_CHUNK), 0)
    col = lax.broadcasted_iota(jnp.int32, (CUM_CHUNK, CUM_CHUNK), 1)
    tri = jnp.where(row >= col, 1.0, 0.0).astype(BF16)
    lane = lax.broadcasted_iota(jnp.int32, (1, LANES), 1)
    group = jnp.right_shift(lane, FOX_HEADS.bit_length() - 1)
    one = jnp.ones((CUM_CHUNK, LANES), BF16)
    carry = jnp.zeros((1, LANES), F32)
    for ci in range(seq // CUM_CHUNK):
        z = f_ref[ci * CUM_CHUNK:(ci + 1) * CUM_CHUNK, :] + b_ref[...]
        logf = jnp.minimum(z, 0.0) - jnp.log(1.0 + jnp.exp(-jnp.abs(z)))
        c = carry
        for part in _split3(logf):
            c = c + jnp.dot(tri, part, preferred_element_type=F32)
        carry = c[CUM_CHUNK - 1:CUM_CHUNK, :]
        hi, mid, lo = _split3(c)
        o_ref[ci * CUM_CHUNK:(ci + 1) * CUM_CHUNK, :] = jnp.where(
            lane == ONES_LANE, one, jnp.where(group == 0, hi, jnp.where(group == 1, mid, lo)))


def _forget_cumsum(side, b_pad, *, batch, seq):
    side3 = side.reshape(batch, seq, SIDE_WIDTH)
    return pl.pallas_call(
        _forget_cumsum_kernel,
        grid=(batch,),
        in_specs=[
            pl.BlockSpec((None, seq, LANES), lambda b: (b, 0, POOL_WIDTH // LANES)),
            _const_spec((1, LANES)),
        ],
        out_specs=pl.BlockSpec((None, seq, LANES), lambda b: (b, 0, 0)),
        out_shape=jax.ShapeDtypeStruct((batch, seq, LANES), BF16),
        compiler_params=pltpu.CompilerParams(
            dimension_semantics=("parallel",), vmem_limit_bytes=VMEM_LIMIT),
        name="forget_cumsum",
    )(side3, b_pad)


FOX_BLOCK = 256


FOX_PAIRS_PER_STEP = 2
FOX_SCRATCH_PER_PAIR = 7


def _fox_kernel(q_ref, k_ref, v_ref, c_ref, qg_ref, kg_ref, o_ref, *scratch):
    for i in range(FOX_PAIRS_PER_STEP):
        lanes = slice(i * LANES, (i + 1) * LANES)
        _fox_pair(pl.program_id(1) * FOX_PAIRS_PER_STEP + i,
                  q_ref.at[:, lanes], k_ref.at[:, lanes], v_ref.at[:, lanes], c_ref,
                  qg_ref, kg_ref, o_ref.at[:, lanes],
                  *scratch[i * FOX_SCRATCH_PER_PAIR:(i + 1) * FOX_SCRATCH_PER_PAIR])


def _fox_pair(pair, q_ref, k_ref, v_ref, c_ref, qg_ref, kg_ref, o_ref,
              qa_ref, qb_ref, kat_ref, kbt_ref, va_ref, vb_ref, extra_ref):
    seq = q_ref.shape[0]
    blk = FOX_BLOCK
    nblk = seq // blk
    lane = lax.broadcasted_iota(jnp.int32, (1, LANES), 1)
    head_a = lane < FOX_HEAD_DIM
    head_b = jnp.logical_not(head_a)

    r = lax.broadcasted_iota(jnp.int32, (LANES, 4 * LANES), 0)
    c = lax.broadcasted_iota(jnp.int32, (LANES, 4 * LANES), 1)
    operand = jnp.right_shift(c, LANES.bit_length() - 1)
    is_query = operand < 2
    second = jnp.bitwise_and(operand, 1) == 1
    feat_lane = jnp.bitwise_and(c, LANES - 1) - jnp.where(second, 0, FOX_HEAD_DIM)
    var_lane = feat_lane - jnp.where(is_query, 0, N_PARTS)
    one_lane = feat_lane - jnp.where(is_query, N_PARTS, 0)
    part = jnp.right_shift(r, FOX_HEADS.bit_length() - 1)
    part_row = ((part < N_PARTS) & (var_lane == part)
                & (jnp.bitwise_and(r, FOX_HEADS - 1) == 2 * pair + second.astype(jnp.int32)))
    one_row = (r == ONES_LANE) & (one_lane >= 0) & (one_lane < N_PARTS)
    sel = jnp.where(part_row, jnp.where(is_query, 1.0, -1.0),
                    jnp.where(one_row, 1.0, 0.0)).astype(BF16)

    q_gain = qg_ref[...] * FOX_HEAD_DIM ** -0.5
    k_gain = kg_ref[...]
    gains = ((jnp.where(head_a, q_gain, 0.0), jnp.where(head_b, q_gain, 0.0)),
             (jnp.where(head_a, k_gain, 0.0), jnp.where(head_b, k_gain, 0.0)))
    ones_a = jnp.broadcast_to(jnp.where(head_a, 1.0, 0.0).astype(BF16), (blk, LANES))
    ones_b = jnp.broadcast_to(jnp.where(head_b, 1.0, 0.0).astype(BF16), (blk, LANES))

    extra_ref[...] = jnp.dot(c_ref[...], sel, preferred_element_type=F32).astype(BF16)

    def head_rms_scale(x):
        sq = x * x
        s_a = jnp.sum(jnp.where(head_a, sq, 0.0), axis=-1, keepdims=True)
        s_b = jnp.sum(jnp.where(head_a, 0.0, sq), axis=-1, keepdims=True)
        return lax.rsqrt(jnp.where(head_a, s_a, s_b) * (1.0 / FOX_HEAD_DIM) + EPS)

    def prepare(rows):
        q = q_ref[rows, :].astype(F32)
        k = k_ref[rows, :].astype(F32)
        extra = extra_ref[rows, :].astype(F32)
        qn = q * head_rms_scale(q)
        kn = k * head_rms_scale(k)
        for hd, (q_aug, kt_aug) in enumerate(((qa_ref, kat_ref), (qb_ref, kbt_ref))):
            q_aug[rows, :] = (qn * gains[0][hd]
                              + extra[:, hd * LANES:(hd + 1) * LANES]).astype(BF16)
            k_aug = kn * gains[1][hd] + extra[:, (2 + hd) * LANES:(3 + hd) * LANES]
            kt_aug[:, rows] = k_aug.T.astype(BF16)
        v = v_ref[rows, :]
        zero = jnp.zeros_like(v)
        va_ref[rows, 0:LANES] = jnp.where(head_a, v, zero)
        vb_ref[rows, 0:LANES] = jnp.where(head_b, v, zero)
        va_ref[rows, LANES:] = ones_a
        vb_ref[rows, LANES:] = ones_b

    row = lax.broadcasted_iota(jnp.int32, (blk, blk), 0)
    col = lax.broadcasted_iota(jnp.int32, (blk, blk), 1)
    causal = row >= col

    prepare(slice(0, blk))
    for qi in range(nblk):
        q_rows = slice(qi * blk, (qi + 1) * blk)
        past = slice(0, qi * blk)
        if qi + 1 < nblk:
            prepare(slice((qi + 1) * blk, (qi + 2) * blk))
        acc = None
        for q_aug, kt_aug, v_aug in ((qa_ref, kat_ref, va_ref), (qb_ref, kbt_ref, vb_ref)):
            q = q_aug[q_rows, :]
            s_diag = jnp.dot(q, kt_aug[:, q_rows], preferred_element_type=F32)
            s_diag = jnp.where(causal, s_diag, -jnp.inf)
            m = jnp.max(s_diag, axis=-1, keepdims=True)
            if qi > 0:
                s_past = jnp.dot(q, kt_aug[:, past], preferred_element_type=F32)
                m = jnp.maximum(m, jnp.max(s_past, axis=-1, keepdims=True))
                p_past = jnp.exp(s_past - m).astype(BF16)
                part = jnp.dot(p_past, v_aug[past, :], preferred_element_type=F32)
                acc = part if acc is None else acc + part
            p_diag = jnp.exp(s_diag - m).astype(BF16)
            part = jnp.dot(p_diag, v_aug[q_rows, :], preferred_element_type=F32)
            acc = part if acc is None else acc + part
        o_ref[q_rows, :] = (acc[:, :LANES] / acc[:, LANES:]).astype(BF16)


def _fox_attention(main3, c_all, q_gain2, k_gain2):
    batch, seq, _ = main3.shape
    tile = pltpu.VMEM((seq, LANES), BF16)
    tile_t = pltpu.VMEM((LANES, seq), BF16)
    wide = pltpu.VMEM((seq, 2 * LANES), BF16)
    width = FOX_PAIRS_PER_STEP * LANES
    qc, kc, vc = Q_COL // width, K_COL // width, V_COL // width
    per_pair = [tile, tile, tile_t, tile_t, wide, wide, pltpu.VMEM((seq, 4 * LANES), BF16)]
    assert len(per_pair) == FOX_SCRATCH_PER_PAIR
    return pl.pallas_call(
        _fox_kernel,
        grid=(batch, HEAD_PAIRS // FOX_PAIRS_PER_STEP),
        in_specs=[
            pl.BlockSpec((None, seq, width), lambda b, p: (b, 0, qc + p)),
            pl.BlockSpec((None, seq, width), lambda b, p: (b, 0, kc + p)),
            pl.BlockSpec((None, seq, width), lambda b, p: (b, 0, vc + p)),
            pl.BlockSpec((None, seq, LANES), lambda b, p: (b, 0, 0)),
            _const_spec((1, LANES)),
            _const_spec((1, LANES)),
        ],
        out_specs=pl.BlockSpec((None, seq, width), lambda b, p: (b, 0, p)),
        out_shape=jax.ShapeDtypeStruct((batch, seq, FOX_WIDTH), BF16),
        scratch_shapes=per_pair * FOX_PAIRS_PER_STEP,
        compiler_params=pltpu.CompilerParams(
            dimension_semantics=("parallel", "parallel"),
            vmem_limit_bytes=VMEM_LIMIT),
        name="fox_attention",
    )(main3, main3, main3, c_all, q_gain2, k_gain2)


MIX_SLAB = 512


def _mix_kernel(seq, u_ref, halo_ref, gp_ref, gf_ref, gm_ref, qm_ref,
                km_ref, vm_ref, of_ref, pw_ref, ps_ref, wpu_ref, wfo_ref,
                wmo_ref, qg_ref, o_ref, ubuf_ref):
    tm = u_ref.shape[0]
    pos0 = (pl.program_id(0) * tm) % seq

    n_slabs = D_MODEL // MIX_SLAB
    slab = lambda s: slice(s * MIX_SLAB, (s + 1) * MIX_SLAB)

    u = u_ref[...]
    halo_valid = (pos0 > 0).astype(F32)
    ubuf_ref[0:POOL_HALO, :] = halo_ref[...] * halo_valid
    ubuf_ref[POOL_HALO:, :] = u
    pos = pos0 + lax.broadcasted_iota(jnp.int32, (tm, 1), 0)

    def pool_group(g):
        win = POOL_WINDOWS[g]
        sl = slice(g * POOL_GROUP_DIM, (g + 1) * POOL_GROUP_DIM)
        tot = u[:, sl]
        for d in range(1, win):
            tot = tot + ubuf_ref[POOL_HALO - d:POOL_HALO - d + tm, sl]
        count = jnp.minimum(pos + 1, win).astype(F32)
        diff = tot / count - u[:, sl]
        mg = jnp.dot(diff.astype(BF16), pw_ref[g], preferred_element_type=F32)
        return (mg * ps_ref[:, sl]).astype(BF16)

    nt = (((1,), (1,)), ((), ()))

    def mem_head(hd):
        sl = slice(hd * MEM_HEAD_DIM, (hd + 1) * MEM_HEAD_DIM)
        q = qm_ref[:, sl].astype(F32)
        qn = (q * _rms_scale(q) * qg_ref[...]).astype(BF16)
        s = lax.dot_general(qn, km_ref[:, sl], nt, preferred_element_type=F32)
        s = s * (MEM_HEAD_DIM ** -0.5)
        p = jnp.exp(s - jnp.max(s, axis=-1, keepdims=True))
        p = p / jnp.sum(p, axis=-1, keepdims=True)
        return jnp.dot(p.astype(BF16), vm_ref[:, sl],
                       preferred_element_type=F32).astype(BF16)

    mixed = jnp.concatenate([pool_group(g) for g in range(POOL_GROUPS)], axis=-1)
    heads = jnp.concatenate([mem_head(hd) for hd in range(MEM_HEADS)], axis=-1)
    o_fox = of_ref[...]

    def gate(ref, cols):
        return 0.5 * jnp.tanh(0.5 * ref[:, cols].astype(F32)) + 0.5

    for s in range(n_slabs):
        cols = slab(s)
        y_pool = jnp.dot(mixed, wpu_ref[:, cols], preferred_element_type=F32)
        y_fox = jnp.dot(o_fox, wfo_ref[:, cols], preferred_element_type=F32)
        y_mem = jnp.dot(heads, wmo_ref[:, cols], preferred_element_type=F32)
        merged = (gate(gp_ref, cols) * y_pool + gate(gf_ref, cols) * y_fox
                  + gate(gm_ref, cols) * y_mem)
        o_ref[:, cols] = merged.astype(BF16)


def _mix(side, main, q_m, km, vm, o_fox, pool_w, pool_scale, w_pool_up, w_fox_o,
         w_mem_o, q_gain, *, seq, tm=512):
    m = side.shape[0]
    mem_len = km.shape[1]
    halo_blocks = tm // POOL_HALO
    gate_blk = D_MODEL
    return pl.pallas_call(
        functools.partial(_mix_kernel, seq),
        grid=(m // tm,),
        in_specs=[
            pl.BlockSpec((tm, POOL_WIDTH), lambda i: (i, 0)),
            pl.BlockSpec((POOL_HALO, POOL_WIDTH),
                         lambda i: (jnp.maximum(i * halo_blocks - 1, 0), 0)),
            pl.BlockSpec((tm, gate_blk), lambda i: (i, 0)),
            pl.BlockSpec((tm, gate_blk), lambda i: (i, 1)),
            pl.BlockSpec((tm, gate_blk), lambda i: (i, 2)),
            pl.BlockSpec((tm, MEM_WIDTH), lambda i: (i, 0)),
            pl.BlockSpec((None, mem_len, MEM_WIDTH), lambda i: ((i * tm) // seq, 0, 0)),
            pl.BlockSpec((None, mem_len, MEM_WIDTH), lambda i: ((i * tm) // seq, 0, 0)),
            pl.BlockSpec((tm, FOX_WIDTH), lambda i: (i, 0)),
            _const_spec((POOL_GROUPS, POOL_GROUP_DIM, POOL_GROUP_DIM)),
            _const_spec((1, POOL_WIDTH)),
            _const_spec((POOL_WIDTH, D_MODEL)),
            _const_spec((FOX_WIDTH, D_MODEL)),
            _const_spec((MEM_WIDTH, D_MODEL)),
            _const_spec((1, MEM_HEAD_DIM)),
        ],
        out_specs=pl.BlockSpec((tm, D_MODEL), lambda i: (i, 0)),
        out_shape=jax.ShapeDtypeStruct((m, D_MODEL), BF16),
        scratch_shapes=[pltpu.VMEM((tm + POOL_HALO, POOL_WIDTH), F32)],
        compiler_params=pltpu.CompilerParams(
            dimension_semantics=("parallel",), vmem_limit_bytes=VMEM_LIMIT),
        name="mix",
    )(side, side, main, main, main, q_m, km, vm, o_fox, pool_w, pool_scale,
      w_pool_up, w_fox_o, w_mem_o, q_gain)


def _outproj_kernel(x_ref, m_ref, w_ref, o_ref):
    o_ref[...] = x_ref[...] + jnp.dot(m_ref[...], w_ref[...], preferred_element_type=F32)


def _outproj(x, merged, w_out, *, tm=512):
    m = x.shape[0]
    return pl.pallas_call(
        _outproj_kernel,
        grid=(m // tm,),
        in_specs=[
            pl.BlockSpec((tm, D_MODEL), lambda i: (i, 0)),
            pl.BlockSpec((tm, D_MODEL), lambda i: (i, 0)),
            _const_spec((D_MODEL, D_MODEL)),
        ],
        out_specs=pl.BlockSpec((tm, D_MODEL), lambda i: (i, 0)),
        out_shape=jax.ShapeDtypeStruct((m, D_MODEL), F32),
        compiler_params=pltpu.CompilerParams(
            dimension_semantics=("parallel",), vmem_limit_bytes=VMEM_LIMIT),
        name="outproj",
    )(x, merged, w_out)


def kernel(x, mem, ffn1_norm, ffn1_w_gate_up, ffn1_w_down, mix_norm, mem_norm, w_in,
           b_forget, pool_w, pool_scale, w_pool_up, fox_q_norm, fox_k_norm, w_fox_o,
           w_mem_kv, mem_q_norm, mem_k_norm, w_mem_o, w_out,
           ffn2_norm, ffn2_w_gate_up, ffn2_w_down):
    batch, seq, _ = x.shape
    mem_len = mem.shape[1]
    depth = w_in.shape[0]
    xf = x.reshape(batch * seq, D_MODEL)
    memf = mem.reshape(batch * mem_len, D_MODEL)
    row = lambda v: v.reshape(1, -1).astype(F32)

    c_f = POOL_WIDTH + 3 * FOX_WIDTH

    for l in range(depth):
        wt = jnp.swapaxes(w_in[l], 0, 1).astype(BF16)
        w_side_t = jnp.pad(
            jnp.concatenate([wt[:POOL_WIDTH]] + [wt[c_f:c_f + FOX_HEADS]] * N_PARTS, axis=0),
            ((0, LANES - N_PARTS * FOX_HEADS), (0, 0)))
        b_pad = jnp.pad(jnp.tile(b_forget[l].astype(F32), N_PARTS),
                        (0, LANES - N_PARTS * FOX_HEADS)).reshape(1, LANES)

        xf, w2_gate_up, w2_down, w_out_b, w_fox_o_b = _ffn(
            xf, row(ffn1_norm[l]), ffn1_w_gate_up[l].astype(BF16),
            ffn1_w_down[l].astype(BF16),
            cast_along=(ffn2_w_gate_up[l], ffn2_w_down[l], w_out[l], w_fox_o[l]))

        main, side, q_m = _inproj(xf, row(mix_norm[l]), wt, w_side_t)
        km, vm = _memkv(memf, row(mem_norm[l]), w_mem_kv[l].astype(BF16),
                        row(mem_k_norm[l]))

        c_all = _forget_cumsum(side, b_pad, batch=batch, seq=seq)
        o_fox = _fox_attention(main.reshape(batch, seq, MAIN_WIDTH), c_all,
                               row(jnp.tile(fox_q_norm[l], 2)),
                               row(jnp.tile(fox_k_norm[l], 2)))

        merged = _mix(side, main, q_m, km.reshape(batch, mem_len, MEM_WIDTH),
                      vm.reshape(batch, mem_len, MEM_WIDTH),
                      o_fox.reshape(batch * seq, FOX_WIDTH),
                      pool_w[l].astype(BF16), row(pool_scale[l]),
                      w_pool_up[l].astype(BF16), w_fox_o_b,
                      w_mem_o[l].astype(BF16), row(mem_q_norm[l]), seq=seq)
        xf = _outproj(xf, merged, w_out_b)

        xf = _ffn(xf, row(ffn2_norm[l]), w2_gate_up, w2_down)
    return xf.reshape(batch, seq, D_MODEL)
```

```python
import functools

import jax
import jax.numpy as jnp
from jax import lax
from jax.experimental import pallas as pl
from jax.experimental.pallas import tpu as pltpu

F32 = jnp.float32
BF16 = jnp.bfloat16

D_MODEL = 2048
D_FF = 5632
POOL_GROUPS = 4
POOL_GROUP_DIM = 128
POOL_WIDTH = POOL_GROUPS * POOL_GROUP_DIM
POOL_WINDOWS = (2, 4, 8, 16)
POOL_HALO = 16
FOX_HEADS = 16
FOX_HEAD_DIM = 64
FOX_WIDTH = FOX_HEADS * FOX_HEAD_DIM
MEM_HEADS = 4
MEM_HEAD_DIM = 128
MEM_WIDTH = MEM_HEADS * MEM_HEAD_DIM
GATE_WIDTH = 3 * D_MODEL
EPS = 1e-6

LANES = 128
HEAD_PAIRS = FOX_WIDTH // LANES

MAIN_WIDTH = GATE_WIDTH + 3 * FOX_WIDTH
Q_COL, K_COL, V_COL = GATE_WIDTH, GATE_WIDTH + FOX_WIDTH, GATE_WIDTH + 2 * FOX_WIDTH
SIDE_WIDTH = POOL_WIDTH + LANES

VMEM_LIMIT = 56 * 1024 * 1024
FFN_VMEM_LIMIT = 58 * 1024 * 1024


def _rms_scale(x):
    return lax.rsqrt(jnp.mean(x * x, axis=-1, keepdims=True) + EPS)


def _const_spec(shape):
    n = len(shape)
    return pl.BlockSpec(shape, lambda *_: (0,) * n, pipeline_mode=pl.Buffered(1))


BF16_SUBLANES = 16
CAST_BAND_ROWS = 2 * BF16_SUBLANES


def _ffn_kernel(n_cast, x_ref, g_ref, wg_ref, wu_ref, wd_ref, *refs):
    cast_in, o_ref, cast_out, h_ref = (refs[:n_cast], refs[n_cast],
                                       refs[n_cast + 1:2 * n_cast + 1], refs[-1])
    j = pl.program_id(1)

    @pl.when(j == 0)
    def _():
        x = x_ref[...]
        h_ref[...] = (x * _rms_scale(x) * g_ref[...]).astype(BF16)
        o_ref[...] = x

    h = h_ref[...]
    gate = jnp.dot(h, wg_ref[...], preferred_element_type=F32)
    for src, dst in zip(cast_in, cast_out):
        dst[...] = src[...].astype(BF16)
    up = jnp.dot(h, wu_ref[...], preferred_element_type=F32)
    act = (0.5 * gate * jax.nn.sigmoid(gate) * up).astype(BF16)
    o_ref[...] += jnp.dot(act, wd_ref[...], preferred_element_type=F32)


def _cast_block_spec(shape, ni, nj):
    rows, cols = shape
    steps = ni * nj
    if rows % (steps * BF16_SUBLANES) == 0:
        return pl.BlockSpec((rows // steps, cols), lambda i, j: (i * nj + j, 0))
    if (rows % (ni * BF16_SUBLANES) == 0 and cols % (nj * LANES) == 0):
        return pl.BlockSpec((rows // ni, cols // nj), lambda i, j: (i, j))
    if rows % (ni * BF16_SUBLANES) == 0:
        return pl.BlockSpec((rows // ni, cols), lambda i, j: (i, 0))
    bands = pl.cdiv(rows, CAST_BAND_ROWS)
    assert bands <= steps, shape
    return pl.BlockSpec((CAST_BAND_ROWS, cols),
                        lambda i, j: (jnp.minimum(i * nj + j, bands - 1), 0))


def _ffn(x, norm_g, w_gate_up, w_down, cast_along=(), *, tm=1024, tf=512):
    m = x.shape[0]
    ni, nf = m // tm, D_FF // tf
    cast_specs = [_cast_block_spec(w.shape, ni, nf) for w in cast_along]
    out = pl.pallas_call(
        functools.partial(_ffn_kernel, len(cast_along)),
        grid=(ni, nf),
        in_specs=[
            pl.BlockSpec((tm, D_MODEL), lambda i, j: (i, 0)),
            _const_spec((1, D_MODEL)),
            pl.BlockSpec((D_MODEL, tf), lambda i, j: (0, j)),
            pl.BlockSpec((D_MODEL, tf), lambda i, j: (0, j + nf)),
            pl.BlockSpec((tf, D_MODEL), lambda i, j: (j, 0)),
        ] + cast_specs,
        out_specs=[pl.BlockSpec((tm, D_MODEL), lambda i, j: (i, 0))] + cast_specs,
        out_shape=[jax.ShapeDtypeStruct((m, D_MODEL), F32)]
        + [jax.ShapeDtypeStruct(w.shape, BF16) for w in cast_along],
        scratch_shapes=[pltpu.VMEM((tm, D_MODEL), BF16)],
        compiler_params=pltpu.CompilerParams(
            dimension_semantics=("parallel", "arbitrary"),
            vmem_limit_bytes=FFN_VMEM_LIMIT),
        name="ffn",
    )(x, norm_g, w_gate_up, w_gate_up, w_down, *cast_along)
    return out[0] if len(out) == 1 else out


def _inproj_kernel(x_ref, g_ref, w_ref, ws_ref, wq_ref, o_ref, os_ref, oq_ref, h_ref):
    j = pl.program_id(1)
    nt = (((1,), (1,)), ((), ()))

    @pl.when(j == 0)
    def _():
        x = x_ref[...]
        h_ref[...] = (x * _rms_scale(x) * g_ref[...]).astype(BF16)

    h = h_ref[...]
    o_ref[...] = lax.dot_general(h, w_ref[...], nt,
                                 preferred_element_type=F32).astype(BF16)

    @pl.when(j == pl.num_programs(1) - 1)
    def _():
        os_ref[...] = lax.dot_general(h, ws_ref[...], nt, preferred_element_type=F32)
        oq_ref[...] = lax.dot_general(h, wq_ref[...], nt,
                                      preferred_element_type=F32).astype(BF16)


def _inproj(x, norm_g, w_in_t, w_side_t, *, tm=1024, tn=1024):
    m = x.shape[0]
    c_qkv = POOL_WIDTH
    c_qm = POOL_WIDTH + 3 * FOX_WIDTH + FOX_HEADS
    c_gate = c_qm + MEM_WIDTH
    gate_tiles = GATE_WIDTH // tn
    assert c_qkv % BF16_SUBLANES == c_qm % BF16_SUBLANES == c_gate % BF16_SUBLANES == 0

    def main_rows(i, j):
        start = jnp.where(j < gate_tiles, c_gate + j * tn, c_qkv + (j - gate_tiles) * tn)
        return (pl.multiple_of(start, BF16_SUBLANES), 0)

    return pl.pallas_call(
        _inproj_kernel,
        grid=(m // tm, MAIN_WIDTH // tn),
        in_specs=[
            pl.BlockSpec((tm, D_MODEL), lambda i, j: (i, 0)),
            _const_spec((1, D_MODEL)),
            pl.BlockSpec((pl.Element(tn), pl.Element(D_MODEL)), main_rows),
            _const_spec((SIDE_WIDTH, D_MODEL)),
            pl.BlockSpec((pl.Element(MEM_WIDTH), pl.Element(D_MODEL)),
                         lambda i, j: (c_qm, 0), pipeline_mode=pl.Buffered(1)),
        ],
        out_specs=[
            pl.BlockSpec((tm, tn), lambda i, j: (i, j)),
            pl.BlockSpec((tm, SIDE_WIDTH), lambda i, j: (i, 0)),
            pl.BlockSpec((tm, MEM_WIDTH), lambda i, j: (i, 0)),
        ],
        out_shape=[
            jax.ShapeDtypeStruct((m, MAIN_WIDTH), BF16),
            jax.ShapeDtypeStruct((m, SIDE_WIDTH), F32),
            jax.ShapeDtypeStruct((m, MEM_WIDTH), BF16),
        ],
        scratch_shapes=[pltpu.VMEM((tm, D_MODEL), BF16)],
        compiler_params=pltpu.CompilerParams(
            dimension_semantics=("parallel", "arbitrary"),
            vmem_limit_bytes=VMEM_LIMIT),
        name="inproj",
    )(x, norm_g, w_in_t, w_side_t, w_in_t)


def _memkv_kernel(m_ref, g_ref, w_ref, kg_ref, k_ref, v_ref):
    x = m_ref[...]
    h = (x * _rms_scale(x) * g_ref[...]).astype(BF16)
    kv = jnp.dot(h, w_ref[...], preferred_element_type=F32)
    for hd in range(MEM_HEADS):
        sl = slice(hd * MEM_HEAD_DIM, (hd + 1) * MEM_HEAD_DIM)
        kh = kv[:, sl]
        k_ref[:, sl] = (kh * _rms_scale(kh) * kg_ref[...]).astype(BF16)
    v_ref[...] = kv[:, MEM_WIDTH:].astype(BF16)


def _memkv(mem, norm_g, w_kv, k_gain, *, tm=512):
    m = mem.shape[0]
    return pl.pallas_call(
        _memkv_kernel,
        grid=(m // tm,),
        in_specs=[
            pl.BlockSpec((tm, D_MODEL), lambda i: (i, 0)),
            _const_spec((1, D_MODEL)),
            _const_spec((D_MODEL, 2 * MEM_WIDTH)),
            _const_spec((1, MEM_HEAD_DIM)),
        ],
        out_specs=[
            pl.BlockSpec((tm, MEM_WIDTH), lambda i: (i, 0)),
            pl.BlockSpec((tm, MEM_WIDTH), lambda i: (i, 0)),
        ],
        out_shape=[
            jax.ShapeDtypeStruct((m, MEM_WIDTH), BF16),
            jax.ShapeDtypeStruct((m, MEM_WIDTH), BF16),
        ],
        compiler_params=pltpu.CompilerParams(
            dimension_semantics=("parallel",), vmem_limit_bytes=VMEM_LIMIT),
        name="memkv",
    )(mem, norm_g, w_kv, k_gain)


CUM_CHUNK = 256
N_PARTS = 3
ONES_LANE = N_PARTS * FOX_HEADS


def _split3(x):
    hi = x.astype(BF16)
    r = x - hi.astype(F32)
    mid = r.astype(BF16)
    lo = (r - mid.astype(F32)).astype(BF16)
    return hi, mid, lo


def _forget_cumsum_kernel(f_ref, b_ref, o_ref):
    seq = f_ref.shape[0]
    row = lax.broadcasted_iota(jnp.int32, (CUM_CHUNK, CUM_CHUNK), 0)
    col = lax.broadcasted_iota(jnp.int32, (CUM_CHUNK, CUM_CHUNK), 1)
    tri = jnp.where(row >= col, 1.0, 0.0).astype(BF16)
    lane = lax.broadcasted_iota(jnp.int32, (1, LANES), 1)
    group = jnp.right_shift(lane, FOX_HEADS.bit_length() - 1)
    one = jnp.ones((CUM_CHUNK, LANES), BF16)
    carry = jnp.zeros((1, LANES), F32)
    for ci in range(seq // CUM_CHUNK):
        z = f_ref[ci * CUM_CHUNK:(ci + 1) * CUM_CHUNK, :] + b_ref[...]
        logf = jnp.minimum(z, 0.0) - jnp.log(1.0 + jnp.exp(-jnp.abs(z)))
        c = carry
        for part in _split3(logf):
            c = c + jnp.dot(tri, part, preferred_element_type=F32)
        carry = c[CUM_CHUNK - 1:CUM_CHUNK, :]
        hi, mid, lo = _split3(c)
        o_ref[ci * CUM_CHUNK:(ci + 1) * CUM_CHUNK, :] = jnp.where(
            lane == ONES_LANE, one, jnp.where(group == 0, hi, jnp.where(group == 1, mid, lo)))


def _forget_cumsum(side, b_pad, *, batch, seq):
    side3 = side.reshape(batch, seq, SIDE_WIDTH)
    return pl.pallas_call(
        _forget_cumsum_kernel,
        grid=(batch,),
        in_specs=[
            pl.BlockSpec((None, seq, LANES), lambda b: (b, 0, POOL_WIDTH // LANES)),
            _const_spec((1, LANES)),
        ],
        out_specs=pl.BlockSpec((None, seq, LANES), lambda b: (b, 0, 0)),
        out_shape=jax.ShapeDtypeStruct((batch, seq, LANES), BF16),
        compiler_params=pltpu.CompilerParams(
            dimension_semantics=("parallel",), vmem_limit_bytes=VMEM_LIMIT),
        name="forget_cumsum",
    )(side3, b_pad)


FOX_BLOCK = 256


FOX_PAIRS_PER_STEP = 2
FOX_SCRATCH_PER_PAIR = 7


def _fox_kernel(q_ref, k_ref, v_ref, c_ref, qg_ref, kg_ref, o_ref, *scratch):
    for i in range(FOX_PAIRS_PER_STEP):
        lanes = slice(i * LANES, (i + 1) * LANES)
        _fox_pair(pl.program_id(1) * FOX_PAIRS_PER_STEP + i,
                  q_ref.at[:, lanes], k_ref.at[:, lanes], v_ref.at[:, lanes], c_ref,
                  qg_ref, kg_ref, o_ref.at[:, lanes],
                  *scratch[i * FOX_SCRATCH_PER_PAIR:(i + 1) * FOX_SCRATCH_PER_PAIR])


def _fox_pair(pair, q_ref, k_ref, v_ref, c_ref, qg_ref, kg_ref, o_ref,
              qa_ref, qb_ref, kat_ref, kbt_ref, va_ref, vb_ref, extra_ref):
    seq = q_ref.shape[0]
    blk = FOX_BLOCK
    nblk = seq // blk
    lane = lax.broadcasted_iota(jnp.int32, (1, LANES), 1)
    head_a = lane < FOX_HEAD_DIM
    head_b = jnp.logical_not(head_a)

    r = lax.broadcasted_iota(jnp.int32, (LANES, 4 * LANES), 0)
    c = lax.broadcasted_iota(jnp.int32, (LANES, 4 * LANES), 1)
    operand = jnp.right_shift(c, LANES.bit_length() - 1)
    is_query = operand < 2
    second = jnp.bitwise_and(operand, 1) == 1
    feat_lane = jnp.bitwise_and(c, LANES - 1) - jnp.where(second, 0, FOX_HEAD_DIM)
    var_lane = feat_lane - jnp.where(is_query, 0, N_PARTS)
    one_lane = feat_lane - jnp.where(is_query, N_PARTS, 0)
    part = jnp.right_shift(r, FOX_HEADS.bit_length() - 1)
    part_row = ((part < N_PARTS) & (var_lane == part)
                & (jnp.bitwise_and(r, FOX_HEADS - 1) == 2 * pair + second.astype(jnp.int32)))
    one_row = (r == ONES_LANE) & (one_lane >= 0) & (one_lane < N_PARTS)
    sel = jnp.where(part_row, jnp.where(is_query, 1.0, -1.0),
                    jnp.where(one_row, 1.0, 0.0)).astype(BF16)

    q_gain = qg_ref[...] * FOX_HEAD_DIM ** -0.5
    k_gain = kg_ref[...]
    gains = ((jnp.where(head_a, q_gain, 0.0), jnp.where(head_b, q_gain, 0.0)),
             (jnp.where(head_a, k_gain, 0.0), jnp.where(head_b, k_gain, 0.0)))
    ones_a = jnp.broadcast_to(jnp.where(head_a, 1.0, 0.0).astype(BF16), (blk, LANES))
    ones_b = jnp.broadcast_to(jnp.where(head_b, 1.0, 0.0).astype(BF16), (blk, LANES))

    extra_ref[...] = jnp.dot(c_ref[...], sel, preferred_element_type=F32).astype(BF16)

    def head_rms_scale(x):
        sq = x * x
        s_a = jnp.sum(jnp.where(head_a, sq, 0.0), axis=-1, keepdims=True)
        s_b = jnp.sum(jnp.where(head_a, 0.0, sq), axis=-1, keepdims=True)
        return lax.rsqrt(jnp.where(head_a, s_a, s_b) * (1.0 / FOX_HEAD_DIM) + EPS)

    def prepare(rows):
        q = q_ref[rows, :].astype(F32)
        k = k_ref[rows, :].astype(F32)
        extra = extra_ref[rows, :].astype(F32)
        qn = q * head_rms_scale(q)
        kn = k * head_rms_scale(k)
        for hd, (q_aug, kt_aug) in enumerate(((qa_ref, kat_ref), (qb_ref, kbt_ref))):
            q_aug[rows, :] = (qn * gains[0][hd]
                              + extra[:, hd * LANES:(hd + 1) * LANES]).astype(BF16)
            k_aug = kn * gains[1][hd] + extra[:, (2 + hd) * LANES:(3 + hd) * LANES]
            kt_aug[:, rows] = k_aug.T.astype(BF16)
        v = v_ref[rows, :]
        zero = jnp.zeros_like(v)
        va_ref[rows, 0:LANES] = jnp.where(head_a, v, zero)
        vb_ref[rows, 0:LANES] = jnp.where(head_b, v, zero)
        va_ref[rows, LANES:] = ones_a
        vb_ref[rows, LANES:] = ones_b

    row = lax.broadcasted_iota(jnp.int32, (blk, blk), 0)
    col = lax.broadcasted_iota(jnp.int32, (blk, blk), 1)
    causal = row >= col

    prepare(slice(0, blk))
    for qi in range(nblk):
        q_rows = slice(qi * blk, (qi + 1) * blk)
        past = slice(0, qi * blk)
        if qi + 1 < nblk:
            prepare(slice((qi + 1) * blk, (qi + 2) * blk))
        acc = None
        for q_aug, kt_aug, v_aug in ((qa_ref, kat_ref, va_ref), (qb_ref, kbt_ref, vb_ref)):
            q = q_aug[q_rows, :]
            s_diag = jnp.dot(q, kt_aug[:, q_rows], preferred_element_type=F32)
            s_diag = jnp.where(causal, s_diag, -jnp.inf)
            m = jnp.max(s_diag, axis=-1, keepdims=True)
            if qi > 0:
                s_past = jnp.dot(q, kt_aug[:, past], preferred_element_type=F32)
                m = jnp.maximum(m, jnp.max(s_past, axis=-1, keepdims=True))
                p_past = jnp.exp(s_past - m).astype(BF16)
                part = jnp.dot(p_past, v_aug[past, :], preferred_element_type=F32)
                acc = part if acc is None else acc + part
            p_diag = jnp.exp(s_diag - m).astype(BF16)
            part = jnp.dot(p_diag, v_aug[q_rows, :], preferred_element_type=F32)
            acc = part if acc is None else acc + part
        o_ref[q_rows, :] = (acc[:, :LANES] / acc[:, LANES:]).astype(BF16)


def _fox_attention(main3, c_all, q_gain2, k_gain2):
    batch, seq, _ = main3.shape
    tile = pltpu.VMEM((seq, LANES), BF16)
    tile_t = pltpu.VMEM((LANES, seq), BF16)
    wide = pltpu.VMEM((seq, 2 * LANES), BF16)
    width = FOX_PAIRS_PER_STEP * LANES
    qc, kc, vc = Q_COL // width, K_COL // width, V_COL // width
    per_pair = [tile, tile, tile_t, tile_t, wide, wide, pltpu.VMEM((seq, 4 * LANES), BF16)]
    assert len(per_pair) == FOX_SCRATCH_PER_PAIR
    return pl.pallas_call(
        _fox_kernel,
        grid=(batch, HEAD_PAIRS // FOX_PAIRS_PER_STEP),
        in_specs=[
            pl.BlockSpec((None, seq, width), lambda b, p: (b, 0, qc + p)),
            pl.BlockSpec((None, seq, width), lambda b, p: (b, 0, kc + p)),
            pl.BlockSpec((None, seq, width), lambda b, p: (b, 0, vc + p)),
            pl.BlockSpec((None, seq, LANES), lambda b, p: (b, 0, 0)),
            _const_spec((1, LANES)),
            _const_spec((1, LANES)),
        ],
        out_specs=pl.BlockSpec((None, seq, width), lambda b, p: (b, 0, p)),
        out_shape=jax.ShapeDtypeStruct((batch, seq, FOX_WIDTH), BF16),
        scratch_shapes=per_pair * FOX_PAIRS_PER_STEP,
        compiler_params=pltpu.CompilerParams(
            dimension_semantics=("parallel", "parallel"),
            vmem_limit_bytes=VMEM_LIMIT),
        name="fox_attention",
    )(main3, main3, main3, c_all, q_gain2, k_gain2)


MIX_SLAB = 512


def _mix_kernel(seq, u_ref, halo_ref, gp_ref, gf_ref, gm_ref, qm_ref,
                km_ref, vm_ref, of_ref, pw_ref, ps_ref, wpu_ref, wfo_ref,
                wmo_ref, qg_ref, o_ref, ubuf_ref):
    tm = u_ref.shape[0]
    pos0 = (pl.program_id(0) * tm) % seq

    n_slabs = D_MODEL // MIX_SLAB
    slab = lambda s: slice(s * MIX_SLAB, (s + 1) * MIX_SLAB)

    u = u_ref[...]
    halo_valid = (pos0 > 0).astype(F32)
    ubuf_ref[0:POOL_HALO, :] = halo_ref[...] * halo_valid
    ubuf_ref[POOL_HALO:, :] = u
    pos = pos0 + lax.broadcasted_iota(jnp.int32, (tm, 1), 0)

    def pool_group(g):
        win = POOL_WINDOWS[g]
        sl = slice(g * POOL_GROUP_DIM, (g + 1) * POOL_GROUP_DIM)
        tot = u[:, sl]
        for d in range(1, win):
            tot = tot + ubuf_ref[POOL_HALO - d:POOL_HALO - d + tm, sl]
        count = jnp.minimum(pos + 1, win).astype(F32)
        diff = tot / count - u[:, sl]
        mg = jnp.dot(diff.astype(BF16), pw_ref[g], preferred_element_type=F32)
        return (mg * ps_ref[:, sl]).astype(BF16)

    nt = (((1,), (1,)), ((), ()))

    def mem_head(hd):
        sl = slice(hd * MEM_HEAD_DIM, (hd + 1) * MEM_HEAD_DIM)
        q = qm_ref[:, sl].astype(F32)
        qn = (q * _rms_scale(q) * qg_ref[...]).astype(BF16)
        s = lax.dot_general(qn, km_ref[:, sl], nt, preferred_element_type=F32)
        s = s * (MEM_HEAD_DIM ** -0.5)
        p = jnp.exp(s - jnp.max(s, axis=-1, keepdims=True))
        p = p / jnp.sum(p, axis=-1, keepdims=True)
        return jnp.dot(p.astype(BF16), vm_ref[:, sl],
                       preferred_element_type=F32).astype(BF16)

    mixed = jnp.concatenate([pool_group(g) for g in range(POOL_GROUPS)], axis=-1)
    heads = jnp.concatenate([mem_head(hd) for hd in range(MEM_HEADS)], axis=-1)
    o_fox = of_ref[...]

    def gate(ref, cols):
        return 0.5 * jnp.tanh(0.5 * ref[:, cols].astype(F32)) + 0.5

    for s in range(n_slabs):
        cols = slab(s)
        y_pool = jnp.dot(mixed, wpu_ref[:, cols], preferred_element_type=F32)
        y_fox = jnp.dot(o_fox, wfo_ref[:, cols], preferred_element_type=F32)
        y_mem = jnp.dot(heads, wmo_ref[:, cols], preferred_element_type=F32)
        merged = (gate(gp_ref, cols) * y_pool + gate(gf_ref, cols) * y_fox
                  + gate(gm_ref, cols) * y_mem)
        o_ref[:, cols] = merged.astype(BF16)


def _mix(side, main, q_m, km, vm, o_fox, pool_w, pool_scale, w_pool_up, w_fox_o,
         w_mem_o, q_gain, *, seq, tm=512):
    m = side.shape[0]
    mem_len = km.shape[1]
    halo_blocks = tm // POOL_HALO
    gate_blk = D_MODEL
    return pl.pallas_call(
        functools.partial(_mix_kernel, seq),
        grid=(m // tm,),
        in_specs=[
            pl.BlockSpec((tm, POOL_WIDTH), lambda i: (i, 0)),
            pl.BlockSpec((POOL_HALO, POOL_WIDTH),
                         lambda i: (jnp.maximum(i * halo_blocks - 1, 0), 0)),
            pl.BlockSpec((tm, gate_blk), lambda i: (i, 0)),
            pl.BlockSpec((tm, gate_blk), lambda i: (i, 1)),
            pl.BlockSpec((tm, gate_blk), lambda i: (i, 2)),
            pl.BlockSpec((tm, MEM_WIDTH), lambda i: (i, 0)),
            pl.BlockSpec((None, mem_len, MEM_WIDTH), lambda i: ((i * tm) // seq, 0, 0)),
            pl.BlockSpec((None, mem_len, MEM_WIDTH), lambda i: ((i * tm) // seq, 0, 0)),
            pl.BlockSpec((tm, FOX_WIDTH), lambda i: (i, 0)),
            _const_spec((POOL_GROUPS, POOL_GROUP_DIM, POOL_GROUP_DIM)),
            _const_spec((1, POOL_WIDTH)),
            _const_spec((POOL_WIDTH, D_MODEL)),
            _const_spec((FOX_WIDTH, D_MODEL)),
            _const_spec((MEM_WIDTH, D_MODEL)),
            _const_spec((1, MEM_HEAD_DIM)),
        ],
        out_specs=pl.BlockSpec((tm, D_MODEL), lambda i: (i, 0)),
        out_shape=jax.ShapeDtypeStruct((m, D_MODEL), BF16),
        scratch_shapes=[pltpu.VMEM((tm + POOL_HALO, POOL_WIDTH), F32)],
        compiler_params=pltpu.CompilerParams(
            dimension_semantics=("parallel",), vmem_limit_bytes=VMEM_LIMIT),
        name="mix",
    )(side, side, main, main, main, q_m, km, vm, o_fox, pool_w, pool_scale,
      w_pool_up, w_fox_o, w_mem_o, q_gain)


def _outproj_kernel(x_ref, m_ref, w_ref, o_ref):
    o_ref[...] = x_ref[...] + jnp.dot(m_ref[...], w_ref[...], preferred_element_type=F32)


def _outproj(x, merged, w_out, *, tm=512):
    m = x.shape[0]
    return pl.pallas_call(
        _outproj_kernel,
        grid=(m // tm,),
        in_specs=[
            pl.BlockSpec((tm, D_MODEL), lambda i: (i, 0)),
            pl.BlockSpec((tm, D_MODEL), lambda i: (i, 0)),
            _const_spec((D_MODEL, D_MODEL)),
        ],
        out_specs=pl.BlockSpec((tm, D_MODEL), lambda i: (i, 0)),
        out_shape=jax.ShapeDtypeStruct((m, D_MODEL), F32),
        compiler_params=pltpu.CompilerParams(
            dimension_semantics=("parallel",), vmem_limit_bytes=VMEM_LIMIT),
        name="outproj",
    )(x, merged, w_out)


def kernel(x, mem, ffn1_norm, ffn1_w_gate_up, ffn1_w_down, mix_norm, mem_norm, w_in,
           b_forget, pool_w, pool_scale, w_pool_up, fox_q_norm, fox_k_norm, w_fox_o,
           w_mem_kv, mem_q_norm, mem_k_norm, w_mem_o, w_out,
           ffn2_norm, ffn2_w_gate_up, ffn2_w_down):
    batch, seq, _ = x.shape
    mem_len = mem.shape[1]
    depth = w_in.shape[0]
    xf = x.reshape(batch * seq, D_MODEL)
    memf = mem.reshape(batch * mem_len, D_MODEL)
    row = lambda v: v.reshape(1, -1).astype(F32)

    c_f = POOL_WIDTH + 3 * FOX_WIDTH

    for l in range(depth):
        b_pad = jnp.pad(jnp.tile(b_forget[l].astype(F32), N_PARTS),
                        (0, LANES - N_PARTS * FOX_HEADS)).reshape(1, LANES)

        xf, w2_gate_up, w2_down, w_out_b, wt = _ffn(
            xf, row(ffn1_norm[l]), ffn1_w_gate_up[l].astype(BF16),
            ffn1_w_down[l].astype(BF16),
            cast_along=(ffn2_w_gate_up[l], ffn2_w_down[l], w_out[l],
                        jnp.swapaxes(w_in[l], 0, 1)))
        w_side_t = jnp.pad(
            jnp.concatenate([wt[:POOL_WIDTH]] + [wt[c_f:c_f + FOX_HEADS]] * N_PARTS, axis=0),
            ((0, LANES - N_PARTS * FOX_HEADS), (0, 0)))

        main, side, q_m = _inproj(xf, row(mix_norm[l]), wt, w_side_t)
        km, vm = _memkv(memf, row(mem_norm[l]), w_mem_kv[l].astype(BF16),
                        row(mem_k_norm[l]))

        c_all = _forget_cumsum(side, b_pad, batch=batch, seq=seq)
        o_fox = _fox_attention(main.reshape(batch, seq, MAIN_WIDTH), c_all,
                               row(jnp.tile(fox_q_norm[l], 2)),
                               row(jnp.tile(fox_k_norm[l], 2)))

        merged = _mix(side, main, q_m, km.reshape(batch, mem_len, MEM_WIDTH),
                      vm.reshape(batch, mem_len, MEM_WIDTH),
                      o_fox.reshape(batch * seq, FOX_WIDTH),
                      pool_w[l].astype(BF16), row(pool_scale[l]),
                      w_pool_up[l].astype(BF16), w_fox_o[l].astype(BF16),
                      w_mem_o[l].astype(BF16), row(mem_q_norm[l]), seq=seq)
        xf = _outproj(xf, merged, w_out_b)

        xf = _ffn(xf, row(ffn2_norm[l]), w2_gate_up, w2_down)
    return xf.reshape(batch, seq, D_MODEL)
```

```python
import functools

import jax
import jax.numpy as jnp
from jax import lax
from jax.experimental import pallas as pl
from jax.experimental.pallas import tpu as pltpu

F32 = jnp.float32
BF16 = jnp.bfloat16

D_MODEL = 2048
D_FF = 5632
POOL_GROUPS = 4
POOL_GROUP_DIM = 128
POOL_WIDTH = POOL_GROUPS * POOL_GROUP_DIM
POOL_WINDOWS = (2, 4, 8, 16)
POOL_HALO = 16
FOX_HEADS = 16
FOX_HEAD_DIM = 64
FOX_WIDTH = FOX_HEADS * FOX_HEAD_DIM
MEM_HEADS = 4
MEM_HEAD_DIM = 128
MEM_WIDTH = MEM_HEADS * MEM_HEAD_DIM
GATE_WIDTH = 3 * D_MODEL
EPS = 1e-6

LANES = 128
HEAD_PAIRS = FOX_WIDTH // LANES

MAIN_WIDTH = GATE_WIDTH + 3 * FOX_WIDTH
Q_COL, K_COL, V_COL = GATE_WIDTH, GATE_WIDTH + FOX_WIDTH, GATE_WIDTH + 2 * FOX_WIDTH
SIDE_WIDTH = POOL_WIDTH + LANES

VMEM_LIMIT = 56 * 1024 * 1024
FFN_VMEM_LIMIT = 58 * 1024 * 1024


def _rms_scale(x):
    return lax.rsqrt(jnp.mean(x * x, axis=-1, keepdims=True) + EPS)


def _const_spec(shape):
    n = len(shape)
    return pl.BlockSpec(shape, lambda *_: (0,) * n, pipeline_mode=pl.Buffered(1))


BF16_SUBLANES = 16
CAST_BAND_ROWS = 2 * BF16_SUBLANES


def _ffn_kernel(n_cast, x_ref, g_ref, wg_ref, wu_ref, wd_ref, *refs):
    cast_in, o_ref, cast_out, h_ref = (refs[:n_cast], refs[n_cast],
                                       refs[n_cast + 1:2 * n_cast + 1], refs[-1])
    j = pl.program_id(1)

    @pl.when(j == 0)
    def _():
        x = x_ref[...]
        h_ref[...] = (x * _rms_scale(x) * g_ref[...]).astype(BF16)
        o_ref[...] = x

    h = h_ref[...]
    gate = jnp.dot(h, wg_ref[...], preferred_element_type=F32)
    for src, dst in zip(cast_in, cast_out):
        dst[...] = src[...].astype(BF16)
    up = jnp.dot(h, wu_ref[...], preferred_element_type=F32)
    act = (0.5 * gate * jax.nn.sigmoid(gate) * up).astype(BF16)
    o_ref[...] += jnp.dot(act, wd_ref[...], preferred_element_type=F32)


def _cast_block_spec(shape, ni, nj):
    rows, cols = shape
    steps = ni * nj
    if rows % (steps * BF16_SUBLANES) == 0:
        return pl.BlockSpec((rows // steps, cols), lambda i, j: (i * nj + j, 0))
    if (rows % (ni * BF16_SUBLANES) == 0 and cols % (nj * LANES) == 0):
        return pl.BlockSpec((rows // ni, cols // nj), lambda i, j: (i, j))
    if rows % (ni * BF16_SUBLANES) == 0:
        return pl.BlockSpec((rows // ni, cols), lambda i, j: (i, 0))
    bands = pl.cdiv(rows, CAST_BAND_ROWS)
    assert bands <= steps, shape
    return pl.BlockSpec((CAST_BAND_ROWS, cols),
                        lambda i, j: (jnp.minimum(i * nj + j, bands - 1), 0))


def _ffn(x, norm_g, w_gate_up, w_down, cast_along=(), *, tm=1024, tf=512):
    m = x.shape[0]
    ni, nf = m // tm, D_FF // tf
    cast_specs = [_cast_block_spec(w.shape, ni, nf) for w in cast_along]
    out = pl.pallas_call(
        functools.partial(_ffn_kernel, len(cast_along)),
        grid=(ni, nf),
        in_specs=[
            pl.BlockSpec((tm, D_MODEL), lambda i, j: (i, 0)),
            _const_spec((1, D_MODEL)),
            pl.BlockSpec((D_MODEL, tf), lambda i, j: (0, j)),
            pl.BlockSpec((D_MODEL, tf), lambda i, j: (0, j + nf)),
            pl.BlockSpec((tf, D_MODEL), lambda i, j: (j, 0)),
        ] + cast_specs,
        out_specs=[pl.BlockSpec((tm, D_MODEL), lambda i, j: (i, 0))] + cast_specs,
        out_shape=[jax.ShapeDtypeStruct((m, D_MODEL), F32)]
        + [jax.ShapeDtypeStruct(w.shape, BF16) for w in cast_along],
        scratch_shapes=[pltpu.VMEM((tm, D_MODEL), BF16)],
        compiler_params=pltpu.CompilerParams(
            dimension_semantics=("parallel", "arbitrary"),
            vmem_limit_bytes=FFN_VMEM_LIMIT),
        name="ffn",
    )(x, norm_g, w_gate_up, w_gate_up, w_down, *cast_along)
    return out[0] if len(out) == 1 else out


def _inproj_kernel(x_ref, g_ref, w_ref, ws_ref, wq_ref, o_ref, os_ref, oq_ref, h_ref):
    j = pl.program_id(1)
    nt = (((1,), (1,)), ((), ()))

    @pl.when(j == 0)
    def _():
        x = x_ref[...]
        h_ref[...] = (x * _rms_scale(x) * g_ref[...]).astype(BF16)

    h = h_ref[...]
    o_ref[...] = lax.dot_general(h, w_ref[...], nt,
                                 preferred_element_type=F32).astype(BF16)

    @pl.when(j == pl.num_programs(1) - 1)
    def _():
        os_ref[...] = lax.dot_general(h, ws_ref[...], nt, preferred_element_type=F32)
        oq_ref[...] = lax.dot_general(h, wq_ref[...], nt,
                                      preferred_element_type=F32).astype(BF16)


def _inproj(x, norm_g, w_in_t, w_side_t, *, tm=1024, tn=1024):
    m = x.shape[0]
    c_qkv = POOL_WIDTH
    c_qm = POOL_WIDTH + 3 * FOX_WIDTH + FOX_HEADS
    c_gate = c_qm + MEM_WIDTH
    gate_tiles = GATE_WIDTH // tn
    assert c_qkv % BF16_SUBLANES == c_qm % BF16_SUBLANES == c_gate % BF16_SUBLANES == 0

    def main_rows(i, j):
        start = jnp.where(j < gate_tiles, c_gate + j * tn, c_qkv + (j - gate_tiles) * tn)
        return (pl.multiple_of(start, BF16_SUBLANES), 0)

    return pl.pallas_call(
        _inproj_kernel,
        grid=(m // tm, MAIN_WIDTH // tn),
        in_specs=[
            pl.BlockSpec((tm, D_MODEL), lambda i, j: (i, 0)),
            _const_spec((1, D_MODEL)),
            pl.BlockSpec((pl.Element(tn), pl.Element(D_MODEL)), main_rows),
            _const_spec((SIDE_WIDTH, D_MODEL)),
            pl.BlockSpec((pl.Element(MEM_WIDTH), pl.Element(D_MODEL)),
                         lambda i, j: (c_qm, 0), pipeline_mode=pl.Buffered(1)),
        ],
        out_specs=[
            pl.BlockSpec((tm, tn), lambda i, j: (i, j)),
            pl.BlockSpec((tm, SIDE_WIDTH), lambda i, j: (i, 0)),
            pl.BlockSpec((tm, MEM_WIDTH), lambda i, j: (i, 0)),
        ],
        out_shape=[
            jax.ShapeDtypeStruct((m, MAIN_WIDTH), BF16),
            jax.ShapeDtypeStruct((m, SIDE_WIDTH), F32),
            jax.ShapeDtypeStruct((m, MEM_WIDTH), BF16),
        ],
        scratch_shapes=[pltpu.VMEM((tm, D_MODEL), BF16)],
        compiler_params=pltpu.CompilerParams(
            dimension_semantics=("parallel", "arbitrary"),
            vmem_limit_bytes=VMEM_LIMIT),
        name="inproj",
    )(x, norm_g, w_in_t, w_side_t, w_in_t)


def _memkv_kernel(m_ref, g_ref, w_ref, kg_ref, k_ref, v_ref):
    x = m_ref[...]
    h = (x * _rms_scale(x) * g_ref[...]).astype(BF16)
    kv = jnp.dot(h, w_ref[...], preferred_element_type=F32)
    for hd in range(MEM_HEADS):
        sl = slice(hd * MEM_HEAD_DIM, (hd + 1) * MEM_HEAD_DIM)
        kh = kv[:, sl]
        k_ref[:, sl] = (kh * _rms_scale(kh) * kg_ref[...]).astype(BF16)
    v_ref[...] = kv[:, MEM_WIDTH:].astype(BF16)


def _memkv(mem, norm_g, w_kv, k_gain, *, tm=512):
    m = mem.shape[0]
    return pl.pallas_call(
        _memkv_kernel,
        grid=(m // tm,),
        in_specs=[
            pl.BlockSpec((tm, D_MODEL), lambda i: (i, 0)),
            _const_spec((1, D_MODEL)),
            _const_spec((D_MODEL, 2 * MEM_WIDTH)),
            _const_spec((1, MEM_HEAD_DIM)),
        ],
        out_specs=[
            pl.BlockSpec((tm, MEM_WIDTH), lambda i: (i, 0)),
            pl.BlockSpec((tm, MEM_WIDTH), lambda i: (i, 0)),
        ],
        out_shape=[
            jax.ShapeDtypeStruct((m, MEM_WIDTH), BF16),
            jax.ShapeDtypeStruct((m, MEM_WIDTH), BF16),
        ],
        compiler_params=pltpu.CompilerParams(
            dimension_semantics=("parallel",), vmem_limit_bytes=VMEM_LIMIT),
        name="memkv",
    )(mem, norm_g, w_kv, k_gain)


CUM_CHUNK = 256
N_PARTS = 3
ONES_LANE = N_PARTS * FOX_HEADS


def _split3(x):
    hi = x.astype(BF16)
    r = x - hi.astype(F32)
    mid = r.astype(BF16)
    lo = (r - mid.astype(F32)).astype(BF16)
    return hi, mid, lo


def _forget_cumsum_kernel(f_ref, b_ref, o_ref):
    seq = f_ref.shape[0]
    row = lax.broadcasted_iota(jnp.int32, (CUM_CHUNK, CUM_CHUNK), 0)
    col = lax.broadcasted_iota(jnp.int32, (CUM_CHUNK, CUM_CHUNK), 1)
    tri = jnp.where(row >= col, 1.0, 0.0).astype(BF16)
    lane = lax.broadcasted_iota(jnp.int32, (1, LANES), 1)
    group = jnp.right_shift(lane, FOX_HEADS.bit_length() - 1)
    one = jnp.ones((CUM_CHUNK, LANES), BF16)
    carry = jnp.zeros((1, LANES), F32)
    for ci in range(seq // CUM_CHUNK):
        z = f_ref[ci * CUM_CHUNK:(ci + 1) * CUM_CHUNK, :] + b_ref[...]
        logf = jnp.minimum(z, 0.0) - jnp.log(1.0 + jnp.exp(-jnp.abs(z)))
        c = carry
        for part in _split3(logf):
            c = c + jnp.dot(tri, part, preferred_element_type=F32)
        carry = c[CUM_CHUNK - 1:CUM_CHUNK, :]
        hi, mid, lo = _split3(c)
        o_ref[ci * CUM_CHUNK:(ci + 1) * CUM_CHUNK, :] = jnp.where(
            lane == ONES_LANE, one, jnp.where(group == 0, hi, jnp.where(group == 1, mid, lo)))


def _forget_cumsum(side, b_pad, *, batch, seq):
    side3 = side.reshape(batch, seq, SIDE_WIDTH)
    return pl.pallas_call(
        _forget_cumsum_kernel,
        grid=(batch,),
        in_specs=[
            pl.BlockSpec((None, seq, LANES), lambda b: (b, 0, POOL_WIDTH // LANES)),
            _const_spec((1, LANES)),
        ],
        out_specs=pl.BlockSpec((None, seq, LANES), lambda b: (b, 0, 0)),
        out_shape=jax.ShapeDtypeStruct((batch, seq, LANES), BF16),
        compiler_params=pltpu.CompilerParams(
            dimension_semantics=("parallel",), vmem_limit_bytes=VMEM_LIMIT),
        name="forget_cumsum",
    )(side3, b_pad)


FOX_BLOCK = 256


FOX_PAIRS_PER_STEP = 4
FOX_SCRATCH_PER_PAIR = 7


def _fox_kernel(q_ref, k_ref, v_ref, c_ref, qg_ref, kg_ref, o_ref, *scratch):
    for i in range(FOX_PAIRS_PER_STEP):
        lanes = slice(i * LANES, (i + 1) * LANES)
        _fox_pair(pl.program_id(1) * FOX_PAIRS_PER_STEP + i,
                  q_ref.at[:, lanes], k_ref.at[:, lanes], v_ref.at[:, lanes], c_ref,
                  qg_ref, kg_ref, o_ref.at[:, lanes],
                  *scratch[i * FOX_SCRATCH_PER_PAIR:(i + 1) * FOX_SCRATCH_PER_PAIR])


def _fox_pair(pair, q_ref, k_ref, v_ref, c_ref, qg_ref, kg_ref, o_ref,
              qa_ref, qb_ref, kat_ref, kbt_ref, va_ref, vb_ref, extra_ref):
    seq = q_ref.shape[0]
    blk = FOX_BLOCK
    nblk = seq // blk
    lane = lax.broadcasted_iota(jnp.int32, (1, LANES), 1)
    head_a = lane < FOX_HEAD_DIM
    head_b = jnp.logical_not(head_a)

    r = lax.broadcasted_iota(jnp.int32, (LANES, 4 * LANES), 0)
    c = lax.broadcasted_iota(jnp.int32, (LANES, 4 * LANES), 1)
    operand = jnp.right_shift(c, LANES.bit_length() - 1)
    is_query = operand < 2
    second = jnp.bitwise_and(operand, 1) == 1
    feat_lane = jnp.bitwise_and(c, LANES - 1) - jnp.where(second, 0, FOX_HEAD_DIM)
    var_lane = feat_lane - jnp.where(is_query, 0, N_PARTS)
    one_lane = feat_lane - jnp.where(is_query, N_PARTS, 0)
    part = jnp.right_shift(r, FOX_HEADS.bit_length() - 1)
    part_row = ((part < N_PARTS) & (var_lane == part)
                & (jnp.bitwise_and(r, FOX_HEADS - 1) == 2 * pair + second.astype(jnp.int32)))
    one_row = (r == ONES_LANE) & (one_lane >= 0) & (one_lane < N_PARTS)
    sel = jnp.where(part_row, jnp.where(is_query, 1.0, -1.0),
                    jnp.where(one_row, 1.0, 0.0)).astype(BF16)

    q_gain = qg_ref[...] * FOX_HEAD_DIM ** -0.5
    k_gain = kg_ref[...]
    gains = ((jnp.where(head_a, q_gain, 0.0), jnp.where(head_b, q_gain, 0.0)),
             (jnp.where(head_a, k_gain, 0.0), jnp.where(head_b, k_gain, 0.0)))
    ones_a = jnp.broadcast_to(jnp.where(head_a, 1.0, 0.0).astype(BF16), (blk, LANES))
    ones_b = jnp.broadcast_to(jnp.where(head_b, 1.0, 0.0).astype(BF16), (blk, LANES))

    extra_ref[...] = jnp.dot(c_ref[...], sel, preferred_element_type=F32).astype(BF16)

    def head_rms_scale(x):
        sq = x * x
        s_a = jnp.sum(jnp.where(head_a, sq, 0.0), axis=-1, keepdims=True)
        s_b = jnp.sum(jnp.where(head_a, 0.0, sq), axis=-1, keepdims=True)
        return lax.rsqrt(jnp.where(head_a, s_a, s_b) * (1.0 / FOX_HEAD_DIM) + EPS)

    def prepare(rows):
        q = q_ref[rows, :].astype(F32)
        k = k_ref[rows, :].astype(F32)
        extra = extra_ref[rows, :].astype(F32)
        qn = q * head_rms_scale(q)
        kn = k * head_rms_scale(k)
        for hd, (q_aug, kt_aug) in enumerate(((qa_ref, kat_ref), (qb_ref, kbt_ref))):
            q_aug[rows, :] = (qn * gains[0][hd]
                              + extra[:, hd * LANES:(hd + 1) * LANES]).astype(BF16)
            k_aug = kn * gains[1][hd] + extra[:, (2 + hd) * LANES:(3 + hd) * LANES]
            kt_aug[:, rows] = k_aug.T.astype(BF16)
        v = v_ref[rows, :]
        zero = jnp.zeros_like(v)
        va_ref[rows, 0:LANES] = jnp.where(head_a, v, zero)
        vb_ref[rows, 0:LANES] = jnp.where(head_b, v, zero)
        va_ref[rows, LANES:] = ones_a
        vb_ref[rows, LANES:] = ones_b

    row = lax.broadcasted_iota(jnp.int32, (blk, blk), 0)
    col = lax.broadcasted_iota(jnp.int32, (blk, blk), 1)
    causal = row >= col

    prepare(slice(0, blk))
    for qi in range(nblk):
        q_rows = slice(qi * blk, (qi + 1) * blk)
        past = slice(0, qi * blk)
        if qi + 1 < nblk:
            prepare(slice((qi + 1) * blk, (qi + 2) * blk))
        acc = None
        for q_aug, kt_aug, v_aug in ((qa_ref, kat_ref, va_ref), (qb_ref, kbt_ref, vb_ref)):
            q = q_aug[q_rows, :]
            s_diag = jnp.dot(q, kt_aug[:, q_rows], preferred_element_type=F32)
            s_diag = jnp.where(causal, s_diag, -jnp.inf)
            m = jnp.max(s_diag, axis=-1, keepdims=True)
            if qi > 0:
                s_past = jnp.dot(q, kt_aug[:, past], preferred_element_type=F32)
                m = jnp.maximum(m, jnp.max(s_past, axis=-1, keepdims=True))
                p_past = jnp.exp(s_past - m).astype(BF16)
                part = jnp.dot(p_past, v_aug[past, :], preferred_element_type=F32)
                acc = part if acc is None else acc + part
            p_diag = jnp.exp(s_diag - m).astype(BF16)
            part = jnp.dot(p_diag, v_aug[q_rows, :], preferred_element_type=F32)
            acc = part if acc is None else acc + part
        o_ref[q_rows, :] = (acc[:, :LANES] / acc[:, LANES:]).astype(BF16)


def _fox_attention(main3, c_all, q_gain2, k_gain2):
    batch, seq, _ = main3.shape
    tile = pltpu.VMEM((seq, LANES), BF16)
    tile_t = pltpu.VMEM((LANES, seq), BF16)
    wide = pltpu.VMEM((seq, 2 * LANES), BF16)
    width = FOX_PAIRS_PER_STEP * LANES
    qc, kc, vc = Q_COL // width, K_COL // width, V_COL // width
    per_pair = [tile, tile, tile_t, tile_t, wide, wide, pltpu.VMEM((seq, 4 * LANES), BF16)]
    assert len(per_pair) == FOX_SCRATCH_PER_PAIR
    return pl.pallas_call(
        _fox_kernel,
        grid=(batch, HEAD_PAIRS // FOX_PAIRS_PER_STEP),
        in_specs=[
            pl.BlockSpec((None, seq, width), lambda b, p: (b, 0, qc + p)),
            pl.BlockSpec((None, seq, width), lambda b, p: (b, 0, kc + p)),
            pl.BlockSpec((None, seq, width), lambda b, p: (b, 0, vc + p)),
            pl.BlockSpec((None, seq, LANES), lambda b, p: (b, 0, 0)),
            _const_spec((1, LANES)),
            _const_spec((1, LANES)),
        ],
        out_specs=pl.BlockSpec((None, seq, width), lambda b, p: (b, 0, p)),
        out_shape=jax.ShapeDtypeStruct((batch, seq, FOX_WIDTH), BF16),
        scratch_shapes=per_pair * FOX_PAIRS_PER_STEP,
        compiler_params=pltpu.CompilerParams(
            dimension_semantics=("parallel", "parallel"),
            vmem_limit_bytes=VMEM_LIMIT),
        name="fox_attention",
    )(main3, main3, main3, c_all, q_gain2, k_gain2)


MIX_SLAB = 512


def _mix_kernel(seq, u_ref, halo_ref, gp_ref, gf_ref, gm_ref, qm_ref,
                km_ref, vm_ref, of_ref, pw_ref, ps_ref, wpu_ref, wfo_ref,
                wmo_ref, qg_ref, o_ref, ubuf_ref):
    tm = u_ref.shape[0]
    pos0 = (pl.program_id(0) * tm) % seq

    n_slabs = D_MODEL // MIX_SLAB
    slab = lambda s: slice(s * MIX_SLAB, (s + 1) * MIX_SLAB)

    u = u_ref[...]
    halo_valid = (pos0 > 0).astype(F32)
    ubuf_ref[0:POOL_HALO, :] = halo_ref[...] * halo_valid
    ubuf_ref[POOL_HALO:, :] = u
    pos = pos0 + lax.broadcasted_iota(jnp.int32, (tm, 1), 0)

    def pool_group(g):
        win = POOL_WINDOWS[g]
        sl = slice(g * POOL_GROUP_DIM, (g + 1) * POOL_GROUP_DIM)
        tot = u[:, sl]
        for d in range(1, win):
            tot = tot + ubuf_ref[POOL_HALO - d:POOL_HALO - d + tm, sl]
        count = jnp.minimum(pos + 1, win).astype(F32)
        diff = tot / count - u[:, sl]
        mg = jnp.dot(diff.astype(BF16), pw_ref[g], preferred_element_type=F32)
        return (mg * ps_ref[:, sl]).astype(BF16)

    nt = (((1,), (1,)), ((), ()))

    def mem_head(hd):
        sl = slice(hd * MEM_HEAD_DIM, (hd + 1) * MEM_HEAD_DIM)
        q = qm_ref[:, sl].astype(F32)
        qn = (q * _rms_scale(q) * qg_ref[...]).astype(BF16)
        s = lax.dot_general(qn, km_ref[:, sl], nt, preferred_element_type=F32)
        s = s * (MEM_HEAD_DIM ** -0.5)
        p = jnp.exp(s - jnp.max(s, axis=-1, keepdims=True))
        p = p / jnp.sum(p, axis=-1, keepdims=True)
        return jnp.dot(p.astype(BF16), vm_ref[:, sl],
                       preferred_element_type=F32).astype(BF16)

    mixed = jnp.concatenate([pool_group(g) for g in range(POOL_GROUPS)], axis=-1)
    heads = jnp.concatenate([mem_head(hd) for hd in range(MEM_HEADS)], axis=-1)
    o_fox = of_ref[...]

    def gate(ref, cols):
        return 0.5 * jnp.tanh(0.5 * ref[:, cols].astype(F32)) + 0.5

    for s in range(n_slabs):
        cols = slab(s)
        y_pool = jnp.dot(mixed, wpu_ref[:, cols], preferred_element_type=F32)
        y_fox = jnp.dot(o_fox, wfo_ref[:, cols], preferred_element_type=F32)
        y_mem = jnp.dot(heads, wmo_ref[:, cols], preferred_element_type=F32)
        merged = (gate(gp_ref, cols) * y_pool + gate(gf_ref, cols) * y_fox
                  + gate(gm_ref, cols) * y_mem)
        o_ref[:, cols] = merged.astype(BF16)


def _mix(side, main, q_m, km, vm, o_fox, pool_w, pool_scale, w_pool_up, w_fox_o,
         w_mem_o, q_gain, *, seq, tm=512):
    m = side.shape[0]
    mem_len = km.shape[1]
    halo_blocks = tm // POOL_HALO
    gate_blk = D_MODEL
    return pl.pallas_call(
        functools.partial(_mix_kernel, seq),
        grid=(m // tm,),
        in_specs=[
            pl.BlockSpec((tm, POOL_WIDTH), lambda i: (i, 0)),
            pl.BlockSpec((POOL_HALO, POOL_WIDTH),
                         lambda i: (jnp.maximum(i * halo_blocks - 1, 0), 0)),
            pl.BlockSpec((tm, gate_blk), lambda i: (i, 0)),
            pl.BlockSpec((tm, gate_blk), lambda i: (i, 1)),
            pl.BlockSpec((tm, gate_blk), lambda i: (i, 2)),
            pl.BlockSpec((tm, MEM_WIDTH), lambda i: (i, 0)),
            pl.BlockSpec((None, mem_len, MEM_WIDTH), lambda i: ((i * tm) // seq, 0, 0)),
            pl.BlockSpec((None, mem_len, MEM_WIDTH), lambda i: ((i * tm) // seq, 0, 0)),
            pl.BlockSpec((tm, FOX_WIDTH), lambda i: (i, 0)),
            _const_spec((POOL_GROUPS, POOL_GROUP_DIM, POOL_GROUP_DIM)),
            _const_spec((1, POOL_WIDTH)),
            _const_spec((POOL_WIDTH, D_MODEL)),
            _const_spec((FOX_WIDTH, D_MODEL)),
            _const_spec((MEM_WIDTH, D_MODEL)),
            _const_spec((1, MEM_HEAD_DIM)),
        ],
        out_specs=pl.BlockSpec((tm, D_MODEL), lambda i: (i, 0)),
        out_shape=jax.ShapeDtypeStruct((m, D_MODEL), BF16),
        scratch_shapes=[pltpu.VMEM((tm + POOL_HALO, POOL_WIDTH), F32)],
        compiler_params=pltpu.CompilerParams(
            dimension_semantics=("parallel",), vmem_limit_bytes=VMEM_LIMIT),
        name="mix",
    )(side, side, main, main, main, q_m, km, vm, o_fox, pool_w, pool_scale,
      w_pool_up, w_fox_o, w_mem_o, q_gain)


def _outproj_kernel(x_ref, m_ref, w_ref, o_ref):
    o_ref[...] = x_ref[...] + jnp.dot(m_ref[...], w_ref[...], preferred_element_type=F32)


def _outproj(x, merged, w_out, *, tm=512):
    m = x.shape[0]
    return pl.pallas_call(
        _outproj_kernel,
        grid=(m // tm,),
        in_specs=[
            pl.BlockSpec((tm, D_MODEL), lambda i: (i, 0)),
            pl.BlockSpec((tm, D_MODEL), lambda i: (i, 0)),
            _const_spec((D_MODEL, D_MODEL)),
        ],
        out_specs=pl.BlockSpec((tm, D_MODEL), lambda i: (i, 0)),
        out_shape=jax.ShapeDtypeStruct((m, D_MODEL), F32),
        compiler_params=pltpu.CompilerParams(
            dimension_semantics=("parallel",), vmem_limit_bytes=VMEM_LIMIT),
        name="outproj",
    )(x, merged, w_out)


def kernel(x, mem, ffn1_norm, ffn1_w_gate_up, ffn1_w_down, mix_norm, mem_norm, w_in,
           b_forget, pool_w, pool_scale, w_pool_up, fox_q_norm, fox_k_norm, w_fox_o,
           w_mem_kv, mem_q_norm, mem_k_norm, w_mem_o, w_out,
           ffn2_norm, ffn2_w_gate_up, ffn2_w_down):
    batch, seq, _ = x.shape
    mem_len = mem.shape[1]
    depth = w_in.shape[0]
    xf = x.reshape(batch * seq, D_MODEL)
    memf = mem.reshape(batch * mem_len, D_MODEL)
    row = lambda v: v.reshape(1, -1).astype(F32)

    c_f = POOL_WIDTH + 3 * FOX_WIDTH

    for l in range(depth):
        b_pad = jnp.pad(jnp.tile(b_forget[l].astype(F32), N_PARTS),
                        (0, LANES - N_PARTS * FOX_HEADS)).reshape(1, LANES)

        xf, w2_gate_up, w2_down, w_out_b, wt = _ffn(
            xf, row(ffn1_norm[l]), ffn1_w_gate_up[l].astype(BF16),
            ffn1_w_down[l].astype(BF16),
            cast_along=(ffn2_w_gate_up[l], ffn2_w_down[l], w_out[l],
                        jnp.swapaxes(w_in[l], 0, 1)))
        w_side_t = jnp.pad(
            jnp.concatenate([wt[:POOL_WIDTH]] + [wt[c_f:c_f + FOX_HEADS]] * N_PARTS, axis=0),
            ((0, LANES - N_PARTS * FOX_HEADS), (0, 0)))

        main, side, q_m = _inproj(xf, row(mix_norm[l]), wt, w_side_t)
        km, vm = _memkv(memf, row(mem_norm[l]), w_mem_kv[l].astype(BF16),
                        row(mem_k_norm[l]))

        c_all = _forget_cumsum(side, b_pad, batch=batch, seq=seq)
        o_fox = _fox_attention(main.reshape(batch, seq, MAIN_WIDTH), c_all,
                               row(jnp.tile(fox_q_norm[l], 2)),
                               row(jnp.tile(fox_k_norm[l], 2)))

        merged = _mix(side, main, q_m, km.reshape(batch, mem_len, MEM_WIDTH),
                      vm.reshape(batch, mem_len, MEM_WIDTH),
                      o_fox.reshape(batch * seq, FOX_WIDTH),
                      pool_w[l].astype(BF16), row(pool_scale[l]),
                      w_pool_up[l].astype(BF16), w_fox_o[l].astype(BF16),
                      w_mem_o[l].astype(BF16), row(mem_q_norm[l]), seq=seq)
        xf = _outproj(xf, merged, w_out_b)

        xf = _ffn(xf, row(ffn2_norm[l]), w2_gate_up, w2_down)
    return xf.reshape(batch, seq, D_MODEL)
```
